```python
import jax, jax.numpy as jnp
from jax import lax
import numpy as np

D_MODEL = 2048
BATCH = 2
SEQ = 8192
DEPTH = 4
DEC_BATCH = 32
DEC_SEQ = 16
PAST_LEN = 1024

CHUNK = 64
Q_BLOCK = 128
HEAD_DIM = 128
N_MIX_HEADS = 12
MIX_WIDTH = N_MIX_HEADS * HEAD_DIM
N_MEM_HEADS = 4
MEM_WIDTH = N_MEM_HEADS * HEAD_DIM
BRANCH_WIDTH = MIX_WIDTH + MEM_WIDTH
N_MEM = 256
CONV_WIDTH = 4
CONV_CH = 3 * MIX_WIDTH
N_FOX = (DEPTH + 1) // 2
N_GDN = DEPTH // 2
FOX_IN = 3 * MIX_WIDTH + N_MIX_HEADS + MEM_WIDTH + BRANCH_WIDTH
GDN_IN = 3 * MIX_WIDTH + 2 * N_MIX_HEADS + MEM_WIDTH + BRANCH_WIDTH
EPS = 1e-6

kernel_name = "fox_gdn_hybrid_stream_step"


def rms_norm(x, gain):
    xf = x.astype(jnp.float32)
    y = xf * lax.rsqrt(jnp.mean(xf * xf, axis=-1, keepdims=True) + EPS)
    return (y * gain.astype(jnp.float32)).astype(x.dtype)


def l2_norm(x):
    xf = x.astype(jnp.float32)
    return xf * lax.rsqrt(jnp.sum(xf * xf, axis=-1, keepdims=True) + EPS)


def split_cols(x, sizes):
    idx = [int(s) for s in np.cumsum(sizes)[:-1]]
    return jnp.split(x, idx, axis=-1)


def memory_kv(mem, gain, w_kv):
    b = mem.shape[0]
    kv = rms_norm(mem, gain) @ w_kv
    k, v = jnp.split(kv, 2, axis=-1)
    return (k.reshape(b, N_MEM, N_MEM_HEADS, HEAD_DIM),
            v.reshape(b, N_MEM, N_MEM_HEADS, HEAD_DIM))


def mem_attend(q, k, v):
    s = jnp.einsum('bthd,bmhd->bhtm', q, k).astype(jnp.float32) * (HEAD_DIM ** -0.5)
    p = jax.nn.softmax(s, axis=-1).astype(v.dtype)
    return jnp.einsum('bhtm,bmhd->bthd', p, v)


def branch_out(mix, q_mem, z, mem_k, mem_v, w_out):
    b, T = z.shape[:2]
    qm = q_mem.reshape(b, T, N_MEM_HEADS, HEAD_DIM)
    m = mem_attend(qm, mem_k.astype(qm.dtype), mem_v.astype(qm.dtype))
    br = jnp.concatenate([mix.reshape(b, T, MIX_WIDTH).astype(z.dtype),
                          m.reshape(b, T, MEM_WIDTH).astype(z.dtype)], axis=-1) * jax.nn.silu(z)
    return br @ w_out


def fox_project(proj, b_f):
    b, T = proj.shape[:2]
    qkv, fg, q_mem, z = split_cols(proj, [3 * MIX_WIDTH, N_MIX_HEADS, MEM_WIDTH, BRANCH_WIDTH])
    q, k, v = [t.reshape(b, T, N_MIX_HEADS, HEAD_DIM) for t in jnp.split(qkv, 3, axis=-1)]
    logf = jax.nn.log_sigmoid(fg.astype(jnp.float32) + b_f.astype(jnp.float32))
    return q, k, v, logf, q_mem, z


def fox_attend(q, k, v, Fq, Fk, q_off):
    s = jnp.einsum('bqhd,bkhd->bhqk', q, k).astype(jnp.float32) * (HEAD_DIM ** -0.5)
    s = s + Fq[..., :, None] - Fk[..., None, :]
    qpos = q_off + jnp.arange(q.shape[1])
    kpos = jnp.arange(k.shape[1])
    s = jnp.where(kpos[None, :] <= qpos[:, None], s, -jnp.inf)
    p = jax.nn.softmax(s, axis=-1).astype(v.dtype)
    return jnp.einsum('bhqk,bkhd->bqhd', p, v)


def fox_prompt(q, k, v, logf):
    b, T, H, hd = q.shape
    nb = T // Q_BLOCK
    F = jnp.cumsum(logf, axis=1).transpose(0, 2, 1)
    qb = q.reshape(b, nb, Q_BLOCK, H, hd).transpose(1, 0, 2, 3, 4)
    Fb = F.reshape(b, H, nb, Q_BLOCK).transpose(2, 0, 1, 3)
    offs = jnp.arange(nb) * Q_BLOCK
    o = lax.map(lambda a: fox_attend(a[1], k, v, a[2], F, a[0]), (offs, qb, Fb))
    return o.transpose(1, 0, 2, 3, 4).reshape(b, T, H * hd)


def fox_sample(q, k, v, logf, ck, cv, clogf):
    b, L, H, hd = q.shape
    P = ck.shape[1]
    k_all = jnp.concatenate([ck.astype(k.dtype), k], axis=1)
    v_all = jnp.concatenate([cv.astype(v.dtype), v], axis=1)
    F_all = jnp.cumsum(jnp.concatenate([clogf.astype(jnp.float32), logf], axis=1), axis=1).transpose(0, 2, 1)
    o = fox_attend(q, k_all, v_all, F_all[..., P:], F_all, P)
    return o.reshape(b, L, H * hd)


def causal_conv(x, w, buf):
    T = x.shape[1]
    xp = jnp.concatenate([buf.astype(x.dtype), x], axis=1)
    y = xp[:, 0:T] * w[0]
    for j in range(1, CONV_WIDTH):
        y = y + xp[:, j:j + T] * w[j]
    return jax.nn.silu(y), xp[:, -(CONV_WIDTH - 1):]


def gdn_chunked(q, k, v, g, beta, S0, C):
    b, T, H, dk = q.shape
    dv = v.shape[-1]
    n = T // C
    blk4 = lambda t: t.reshape(b, n, C, H, t.shape[-1]).transpose(1, 0, 3, 2, 4)
    blk3 = lambda t: t.reshape(b, n, C, H).transpose(1, 0, 3, 2)
    qc, kc, vc = blk4(q), blk4(k), blk4(v)
    G = jnp.cumsum(blk3(g), axis=-1)
    bc = blk3(beta)
    tri_incl = jnp.tril(jnp.ones((C, C), bool))
    tri_strict = jnp.tril(jnp.ones((C, C), bool), -1)
    decay = jnp.exp(jnp.where(tri_incl, G[..., :, None] - G[..., None, :], -jnp.inf))
    kk = jnp.einsum('nbhid,nbhjd->nbhij', kc, kc)
    A = jnp.where(tri_strict, bc[..., :, None] * kk * decay, 0.0)
    eye = jnp.eye(C, dtype=jnp.float32)
    rhs = jnp.concatenate([vc * bc[..., None], kc * (bc * jnp.exp(G))[..., None]], axis=-1)
    sol = lax.linalg.triangular_solve(eye + A, rhs, left_side=True, lower=True, unit_diagonal=True)
    u, w = sol[..., :dv], sol[..., dv:]
    qk = jnp.where(tri_incl, jnp.einsum('nbhid,nbhjd->nbhij', qc, kc) * decay, 0.0)
    k_tail = kc * jnp.exp(G[..., -1:] - G)[..., None]
    g_last = jnp.exp(G[..., -1])
    q_dec = qc * jnp.exp(G)[..., None]

    def step(S, xs):
        u_i, w_i, qd_i, qk_i, kt_i, gl_i = xs
        v_new = u_i - jnp.einsum('bhcd,bhde->bhce', w_i, S)
        o = jnp.einsum('bhcd,bhde->bhce', qd_i, S) + jnp.einsum('bhij,bhje->bhie', qk_i, v_new)
        S = S * gl_i[..., None, None] + jnp.einsum('bhcd,bhce->bhde', kt_i, v_new)
        return S, o

    S, o = lax.scan(step, S0, (u, w, q_dec, qk, k_tail, g_last))
    return o.transpose(1, 0, 3, 2, 4).reshape(b, T, H, dv), S


def gdn_mixer(proj, conv_w, A_log, dt_bias, out_gain, S0, buf, C):
    b, T = proj.shape[:2]
    qkv, a, bt, q_mem, z = split_cols(
        proj, [3 * MIX_WIDTH, N_MIX_HEADS, N_MIX_HEADS, MEM_WIDTH, BRANCH_WIDTH])
    qkv_c, new_buf = causal_conv(qkv, conv_w, buf)
    q, k, v = [t.reshape(b, T, N_MIX_HEADS, HEAD_DIM) for t in jnp.split(qkv_c, 3, axis=-1)]
    q = l2_norm(q) * (HEAD_DIM ** -0.5)
    k = l2_norm(k)
    v = v.astype(jnp.float32)
    g = -jnp.exp(A_log.astype(jnp.float32)) * jax.nn.softplus(a.astype(jnp.float32) + dt_bias.astype(jnp.float32))
    beta = jax.nn.sigmoid(bt.astype(jnp.float32))
    o, S = gdn_chunked(q, k, v, g, beta, S0.astype(jnp.float32), C)
    o = rms_norm(o, out_gain).reshape(b, T, MIX_WIDTH)
    return o, q_mem, z, S, new_buf


def setup_inputs(seed: int = 0) -> dict:
    key = jax.random.key(seed)
    ks = jax.random.split(key, 24)
    nrm = lambda k, shape, s: jax.random.normal(k, shape, jnp.float32) * s
    H = N_MIX_HEADS
    x_prompt = nrm(ks[0], (BATCH, SEQ, D_MODEL), 1.0)
    x_sample = nrm(ks[1], (DEC_BATCH, DEC_SEQ, D_MODEL), 1.0)
    mem_prompt = nrm(ks[2], (BATCH, N_MEM, D_MODEL), 1.0)
    cache_fox_k = nrm(ks[3], (N_FOX, DEC_BATCH, PAST_LEN, H, HEAD_DIM), 1.0)
    cache_fox_v = nrm(ks[4], (N_FOX, DEC_BATCH, PAST_LEN, H, HEAD_DIM), 1.0)
    cache_fox_logf = jax.nn.log_sigmoid(
        jax.random.uniform(ks[5], (N_FOX, DEC_BATCH, PAST_LEN, H), jnp.float32, 1.0, 6.0)
        + nrm(ks[6], (N_FOX, DEC_BATCH, PAST_LEN, H), 1.0))
    state_gdn_S = nrm(ks[7], (N_GDN, DEC_BATCH, H, HEAD_DIM, HEAD_DIM), 0.1)
    state_gdn_conv = nrm(ks[8], (N_GDN, DEC_BATCH, CONV_WIDTH - 1, CONV_CH), 1.0)
    cache_mem_k = nrm(ks[9], (DEPTH, DEC_BATCH, N_MEM, N_MEM_HEADS, HEAD_DIM), 1.0)
    cache_mem_v = nrm(ks[10], (DEPTH, DEC_BATCH, N_MEM, N_MEM_HEADS, HEAD_DIM), 1.0)
    norm_gain = 1.0 + nrm(ks[11], (DEPTH, D_MODEL), 0.02)
    mem_norm_gain = 1.0 + nrm(ks[12], (DEPTH, D_MODEL), 0.02)
    w_mem_kv = nrm(ks[13], (DEPTH, D_MODEL, 2 * MEM_WIDTH), D_MODEL ** -0.5)
    w_in_fox = nrm(ks[14], (N_FOX, D_MODEL, FOX_IN), D_MODEL ** -0.5)
    b_forget = jax.random.uniform(ks[15], (N_FOX, H), jnp.float32, 1.0, 6.0)
    w_in_gdn = nrm(ks[16], (N_GDN, D_MODEL, GDN_IN), D_MODEL ** -0.5)
    gdn_conv_w = nrm(ks[17], (N_GDN, CONV_WIDTH, CONV_CH), CONV_WIDTH ** -0.5)
    gdn_A_log = jnp.log(jax.random.uniform(ks[18], (N_GDN, H), jnp.float32, 1.0, 16.0))
    dt = jnp.exp(jax.random.uniform(ks[19], (N_GDN, H), jnp.float32, np.log(1e-3), np.log(1e-1)))
    gdn_dt_bias = dt + jnp.log(-jnp.expm1(-dt))
    gdn_out_norm = 1.0 + nrm(ks[20], (N_GDN, HEAD_DIM), 0.02)
    w_out = nrm(ks[21], (DEPTH, BRANCH_WIDTH, D_MODEL), BRANCH_WIDTH ** -0.5)
    final_norm = 1.0 + nrm(ks[22], (D_MODEL,), 0.02)
    return {"x_prompt": x_prompt, "x_sample": x_sample, "mem_prompt": mem_prompt,
            "cache_fox_k": cache_fox_k, "cache_fox_v": cache_fox_v, "cache_fox_logf": cache_fox_logf,
            "state_gdn_S": state_gdn_S, "state_gdn_conv": state_gdn_conv,
            "cache_mem_k": cache_mem_k, "cache_mem_v": cache_mem_v,
            "norm_gain": norm_gain, "mem_norm_gain": mem_norm_gain, "w_mem_kv": w_mem_kv,
            "w_in_fox": w_in_fox, "b_forget": b_forget, "w_in_gdn": w_in_gdn,
            "gdn_conv_w": gdn_conv_w, "gdn_A_log": gdn_A_log, "gdn_dt_bias": gdn_dt_bias,
            "gdn_out_norm": gdn_out_norm, "w_out": w_out, "final_norm": final_norm}


def reference(x_prompt, x_sample, mem_prompt, cache_fox_k, cache_fox_v, cache_fox_logf,
              state_gdn_S, state_gdn_conv, cache_mem_k, cache_mem_v, norm_gain, mem_norm_gain,
              w_mem_kv, w_in_fox, b_forget, w_in_gdn, gdn_conv_w, gdn_A_log, gdn_dt_bias,
              gdn_out_norm, w_out, final_norm):
    xp, xs = x_prompt, x_sample
    bp, bs = xp.shape[0], xs.shape[0]
    p_k, p_v, p_lf, p_S, p_conv, p_mk, p_mv = [], [], [], [], [], [], []
    s_k, s_v, s_lf, s_S, s_conv = [], [], [], [], []
    for i in range(DEPTH):
        j = i // 2
        mk_p, mv_p = memory_kv(mem_prompt, mem_norm_gain[i], w_mem_kv[i])
        p_mk.append(mk_p)
        p_mv.append(mv_p)
        hp = rms_norm(xp, norm_gain[i])
        hs = rms_norm(xs, norm_gain[i])
        if i % 2 == 0:
            q, k, v, lf, qm_p, z_p = fox_project(hp @ w_in_fox[j], b_forget[j])
            mix_p = fox_prompt(q, k, v, lf)
            p_k.append(k); p_v.append(v); p_lf.append(lf)
            q, k, v, lf, qm_s, z_s = fox_project(hs @ w_in_fox[j], b_forget[j])
            mix_s = fox_sample(q, k, v, lf, cache_fox_k[j], cache_fox_v[j], cache_fox_logf[j])
            s_k.append(k); s_v.append(v); s_lf.append(lf)
        else:
            S0 = jnp.zeros((bp, N_MIX_HEADS, HEAD_DIM, HEAD_DIM), jnp.float32)
            buf0 = jnp.zeros((bp, CONV_WIDTH - 1, CONV_CH), xp.dtype)
            mix_p, qm_p, z_p, S, buf = gdn_mixer(hp @ w_in_gdn[j], gdn_conv_w[j], gdn_A_log[j],
                                                 gdn_dt_bias[j], gdn_out_norm[j], S0, buf0, CHUNK)
            p_S.append(S); p_conv.append(buf)
            mix_s, qm_s, z_s, S, buf = gdn_mixer(hs @ w_in_gdn[j], gdn_conv_w[j], gdn_A_log[j],
                                                 gdn_dt_bias[j], gdn_out_norm[j], state_gdn_S[j],
                                                 state_gdn_conv[j], xs.shape[1])
            s_S.append(S); s_conv.append(buf)
        xp = xp + branch_out(mix_p, qm_p, z_p, mk_p, mv_p, w_out[i])
        xs = xs + branch_out(mix_s, qm_s, z_s, cache_mem_k[i], cache_mem_v[i], w_out[i])
    y_prompt = rms_norm(xp, final_norm)
    y_sample = rms_norm(xs, final_norm)
    return (y_prompt, y_sample,
            jnp.stack(p_k), jnp.stack(p_v), jnp.stack(p_lf), jnp.stack(p_S), jnp.stack(p_conv),
            jnp.stack(p_mk), jnp.stack(p_mv),
            jnp.stack(s_k), jnp.stack(s_v), jnp.stack(s_lf), jnp.stack(s_S), jnp.stack(s_conv))
```

```python
import functools

import jax
import jax.numpy as jnp
from jax import lax
from jax.experimental import pallas as pl
from jax.experimental.pallas import tpu as pltpu

F32 = jnp.float32
BF16 = jnp.bfloat16
HI = lax.Precision.HIGHEST

LANE = 128
HEAD_DIM = 128
N_HEADS = 12
MIX = N_HEADS * HEAD_DIM
N_MEM_HEADS = 4
MEMW = N_MEM_HEADS * HEAD_DIM
CHUNK = 64
EPS = 1e-6
VMEM_LIMIT = 48 * 1024 * 1024

NT_DIMS = (((1,), (1,)), ((), ()))
TN_DIMS = (((0,), (0,)), ((), ()))


def _params(*sem):
    return pltpu.CompilerParams(dimension_semantics=sem, vmem_limit_bytes=VMEM_LIMIT)


def _sigmoid(x):
    return 1.0 / (1.0 + jnp.exp(-x))


def _log_sigmoid(x):
    return jnp.minimum(x, 0.0) - jnp.log1p(jnp.exp(-jnp.abs(x)))


def _softplus(x):
    return jnp.maximum(x, 0.0) + jnp.log1p(jnp.exp(-jnp.abs(x)))


def _dot_hi(a, b):
    return jnp.dot(a, b, precision=HI, preferred_element_type=F32)


def _dot_nt_hi(a, b):
    return lax.dot_general(a, b, NT_DIMS, precision=HI, preferred_element_type=F32)


def _iota2(shape, dim):
    return lax.broadcasted_iota(jnp.int32, shape, dim)


def _rms_kernel(x_ref, g_ref, o_ref):
    x = x_ref[...]
    y = x * lax.rsqrt(jnp.mean(x * x, axis=-1, keepdims=True) + EPS)
    o_ref[...] = (y * g_ref[...]).astype(o_ref.dtype)


def rms_norm_rows(x, gain, out_dtype):
    R, D = x.shape
    tm = min(R, 512)
    return pl.pallas_call(
        _rms_kernel,
        grid=(R // tm,),
        in_specs=[pl.BlockSpec((tm, D), lambda i: (i, 0)),
                  pl.BlockSpec((1, D), lambda i: (0, 0))],
        out_specs=pl.BlockSpec((tm, D), lambda i: (i, 0)),
        out_shape=jax.ShapeDtypeStruct((R, D), out_dtype),
        compiler_params=_params("parallel"),
        name="rms_norm",
    )(x, gain.reshape(1, D))


def _mm_kernel(h_ref, w_ref, *o_refs, kind, scale):
    acc = jnp.dot(h_ref[...], w_ref[...], preferred_element_type=F32)
    if kind == "f32":
        o_refs[0][...] = acc
    elif kind == "scale_bf16":
        o_refs[0][...] = (acc * scale).astype(BF16)
    elif kind == "dual":
        o_refs[0][...] = acc
        o_refs[1][...] = acc.astype(BF16)
    elif kind == "silu_bf16":
        o_refs[0][...] = (acc * _sigmoid(acc)).astype(BF16)
    else:
        raise ValueError(kind)


def mm(h, w, kind, scale=1.0):
    R, K = h.shape
    N = w.shape[1]
    tm = min(R, 1024)
    tn = min(N, 512)
    out_dtypes = {"f32": [F32], "scale_bf16": [BF16], "dual": [F32, BF16], "silu_bf16": [BF16]}[kind]
    outs = pl.pallas_call(
        functools.partial(_mm_kernel, kind=kind, scale=scale),
        grid=(R // tm, N // tn),
        in_specs=[pl.BlockSpec((tm, K), lambda i, n: (i, 0)),
                  pl.BlockSpec((K, tn), lambda i, n: (0, n))],
        out_specs=[pl.BlockSpec((tm, tn), lambda i, n: (i, n)) for _ in out_dtypes],
        out_shape=[jax.ShapeDtypeStruct((R, N), dt) for dt in out_dtypes],
        compiler_params=_params("parallel", "arbitrary"),
        name="proj_" + kind,
    )(h, w)
    return outs if len(outs) > 1 else outs[0]


def _out_proj_kernel(mix_ref, m_ref, g_ref, w_ref, x_ref, o_ref, br_ref):
    @pl.when(pl.program_id(1) == 0)
    def _():
        g = g_ref[...].astype(F32)
        br_ref[:, :MIX] = (mix_ref[...].astype(F32) * g[:, :MIX]).astype(BF16)
        br_ref[:, MIX:] = (m_ref[...].astype(F32) * g[:, MIX:]).astype(BF16)

    o_ref[...] = x_ref[...] + jnp.dot(br_ref[...], w_ref[...], preferred_element_type=F32)


def out_proj(mix, m, g, w, x):
    R, D = x.shape
    BW = MIX + MEMW
    tm = min(R, 1024)
    tn = min(D, 512)
    return pl.pallas_call(
        _out_proj_kernel,
        grid=(R // tm, D // tn),
        in_specs=[pl.BlockSpec((tm, MIX), lambda i, n: (i, 0)),
                  pl.BlockSpec((tm, MEMW), lambda i, n: (i, 0)),
                  pl.BlockSpec((tm, BW), lambda i, n: (i, 0)),
                  pl.BlockSpec((BW, tn), lambda i, n: (0, n)),
                  pl.BlockSpec((tm, tn), lambda i, n: (i, n))],
        out_specs=pl.BlockSpec((tm, tn), lambda i, n: (i, n)),
        out_shape=jax.ShapeDtypeStruct((R, D), F32),
        scratch_shapes=[pltpu.VMEM((tm, BW), BF16)],
        compiler_params=_params("parallel", "arbitrary"),
        name="out_proj",
    )(mix, m, g, w, x)


def _mem_attn_kernel(q_ref, k_ref, v_ref, o_ref):
    for h in range(N_MEM_HEADS):
        sl = slice(h * HEAD_DIM, (h + 1) * HEAD_DIM)
        q = q_ref[:, sl]
        k = k_ref[0, :, sl].astype(BF16)
        v = v_ref[0, :, sl].astype(BF16)
        s = lax.dot_general(q, k, NT_DIMS, preferred_element_type=F32)
        m = jnp.max(s, axis=1, keepdims=True)
        p = jnp.exp(s - m)
        l = jnp.sum(p, axis=1, keepdims=True)
        o = jnp.dot(p.astype(BF16), v, preferred_element_type=F32) / l
        o_ref[:, sl] = o.astype(BF16)


def mem_attn(q, mk, mv, rows_per_batch):
    R = q.shape[0]
    NB, NM, _ = mk.shape
    tr = min(rows_per_batch, 1024)
    nrb = rows_per_batch // tr
    return pl.pallas_call(
        _mem_attn_kernel,
        grid=(NB, nrb),
        in_specs=[pl.BlockSpec((tr, MEMW), lambda b, i: (b * nrb + i, 0)),
                  pl.BlockSpec((1, NM, MEMW), lambda b, i: (b, 0, 0)),
                  pl.BlockSpec((1, NM, MEMW), lambda b, i: (b, 0, 0))],
        out_specs=pl.BlockSpec((tr, MEMW), lambda b, i: (b * nrb + i, 0)),
        out_shape=jax.ShapeDtypeStruct((R, MEMW), BF16),
        compiler_params=_params("parallel", "parallel"),
        name="mem_attn",
    )(q, mk, mv)


def _fox_prep_kernel(fg_ref, bf_ref, logf_ref, fcol_ref, frow_ref, carry_ref, *, tp):
    @pl.when(pl.program_id(1) == 0)
    def _():
        carry_ref[...] = jnp.zeros_like(carry_ref)

    logf = _log_sigmoid(fg_ref[...] + bf_ref[...])
    tril = (_iota2((tp, tp), 1) <= _iota2((tp, tp), 0)).astype(F32)
    F = _dot_hi(tril, logf) + carry_ref[...]
    carry_ref[...] = F[tp - 1:tp, :]
    logf_ref[...] = logf
    fcol_ref[...] = F
    eye = (_iota2((LANE, LANE), 0) == _iota2((LANE, LANE), 1)).astype(F32)
    frow_ref[0, 0] = _dot_nt_hi(eye, F)


def fox_prep(fg, bf_pad, B, T, tp):
    nb = T // tp
    return pl.pallas_call(
        functools.partial(_fox_prep_kernel, tp=tp),
        grid=(B, nb),
        in_specs=[pl.BlockSpec((tp, LANE), lambda b, i: (b * nb + i, 0)),
                  pl.BlockSpec((1, LANE), lambda b, i: (0, 0))],
        out_specs=[pl.BlockSpec((tp, LANE), lambda b, i: (b * nb + i, 0)),
                   pl.BlockSpec((tp, LANE), lambda b, i: (b * nb + i, 0)),
                   pl.BlockSpec((1, 1, LANE, tp), lambda b, i: (b, i, 0, 0))],
        out_shape=[jax.ShapeDtypeStruct((B * T, LANE), F32),
                   jax.ShapeDtypeStruct((B * T, LANE), F32),
                   jax.ShapeDtypeStruct((B, nb, LANE, tp), F32)],
        scratch_shapes=[pltpu.VMEM((1, LANE), F32)],
        compiler_params=_params("parallel", "arbitrary"),
        name="fox_prep",
    )(fg, bf_pad)


def _fox_attn_kernel(q_ref, k_ref, v_ref, fcol_ref, frow_ref, o_ref, *, tq, tk):
    h = pl.program_id(1)
    qi = pl.program_id(2)
    q = q_ref[...]
    sel = _iota2((tq, LANE), 1) == h
    fq = jnp.sum(jnp.where(sel, fcol_ref[...], 0.0), axis=1, keepdims=True)
    row = qi * tq + _iota2((tq, tk), 0)
    col = _iota2((tq, tk), 1)

    def body(kj, carry, masked):
        m, l, acc = carry
        off = pl.multiple_of(kj * tk, tk)
        kb = k_ref[pl.ds(off, tk), :]
        vb = v_ref[pl.ds(off, tk), :]
        s = lax.dot_general(q, kb, NT_DIMS, preferred_element_type=F32)
        fk = frow_ref[0, kj, pl.ds(h, 1), :]
        s = s + (fq - fk)
        if masked:
            s = jnp.where(col + kj * tk <= row, s, -jnp.inf)
        m_new = jnp.maximum(m, jnp.max(s, axis=1, keepdims=True))
        alpha = jnp.exp(m - m_new)
        p = jnp.exp(s - m_new)
        l = alpha * l + jnp.sum(p, axis=1, keepdims=True)
        acc = alpha * acc + jnp.dot(p.astype(BF16), vb, preferred_element_type=F32)
        return m_new, l, acc

    n_full = (qi * tq + 1) // tk
    n_all = ((qi + 1) * tq + tk - 1) // tk
    init = (jnp.full((tq, 1), -jnp.inf, F32), jnp.zeros((tq, 1), F32), jnp.zeros((tq, HEAD_DIM), F32))
    carry = lax.fori_loop(0, n_full, functools.partial(body, masked=False), init)
    m, l, acc = lax.fori_loop(n_full, n_all, functools.partial(body, masked=True), carry)
    o_ref[...] = (acc / l).astype(BF16)


def fox_attn(q, k, v, fcol, frow, B, T, tq, tk):
    nq = T // tq
    nk = T // tk
    return pl.pallas_call(
        functools.partial(_fox_attn_kernel, tq=tq, tk=tk),
        grid=(B, N_HEADS, nq),
        in_specs=[pl.BlockSpec((tq, HEAD_DIM), lambda b, h, i: (b * nq + i, h)),
                  pl.BlockSpec((T, HEAD_DIM), lambda b, h, i: (b, h)),
                  pl.BlockSpec((T, HEAD_DIM), lambda b, h, i: (b, h)),
                  pl.BlockSpec((tq, LANE), lambda b, h, i: (b * nq + i, 0)),
                  pl.BlockSpec((1, nk, LANE, tk), lambda b, h, i: (b, 0, 0, 0))],
        out_specs=pl.BlockSpec((tq, HEAD_DIM), lambda b, h, i: (b * nq + i, h)),
        out_shape=jax.ShapeDtypeStruct((B * T, MIX), BF16),
        compiler_params=_params("parallel", "parallel", "parallel"),
        name="fox_attn",
    )(q, k, v, fcol, frow)


def _suffix_sum_kernel(x_ref, o_ref):
    P = x_ref.shape[1]
    after = (_iota2((P, P), 0) > _iota2((P, P), 1)).astype(F32)
    o_ref[...] = _dot_hi(x_ref[...], after)


def suffix_sum(x):
    return pl.pallas_call(
        _suffix_sum_kernel,
        out_shape=jax.ShapeDtypeStruct(x.shape, F32),
        compiler_params=pltpu.CompilerParams(vmem_limit_bytes=VMEM_LIMIT),
        name="fox_suffix_sum",
    )(x)


def _fox_sample_kernel(q_ref, kn_ref, vn_ref, ck_ref, cv_ref, fg_ref, fgt_ref, bf_ref, bft_ref, r_ref,
                       o_ref, logf_ref, *, L):
    logf = _log_sigmoid(fg_ref[...] + bf_ref[...])
    logf_ref[...] = logf
    r_i = _iota2((L, L), 0)
    c_i = _iota2((L, L), 1)
    causal = c_i <= r_i
    ln_col = _dot_hi(causal.astype(F32), logf)
    logf_t = _log_sigmoid(fgt_ref[0] + bft_ref[...])
    ln_row = _dot_hi(logf_t, (r_i <= c_i).astype(F32))
    for h in range(N_HEADS):
        sl = slice(h * HEAD_DIM, (h + 1) * HEAD_DIM)
        q = q_ref[:, sl]
        kc = ck_ref[0, :, sl].astype(BF16)
        vc = cv_ref[0, :, sl].astype(BF16)
        fq = ln_col[:, h:h + 1]
        sc = lax.dot_general(q, kc, NT_DIMS, preferred_element_type=F32) + (fq + r_ref[0, h:h + 1, :])
        sn = lax.dot_general(q, kn_ref[:, sl], NT_DIMS, preferred_element_type=F32) + (fq - ln_row[h:h + 1, :])
        sn = jnp.where(causal, sn, -jnp.inf)
        m = jnp.maximum(jnp.max(sc, axis=1, keepdims=True), jnp.max(sn, axis=1, keepdims=True))
        pc = jnp.exp(sc - m)
        pn = jnp.exp(sn - m)
        l = jnp.sum(pc, axis=1, keepdims=True) + jnp.sum(pn, axis=1, keepdims=True)
        o = (jnp.dot(pc.astype(BF16), vc, preferred_element_type=F32)
             + jnp.dot(pn.astype(BF16), vn_ref[:, sl], preferred_element_type=F32)) / l
        o_ref[:, sl] = o.astype(BF16)


def fox_sample(q, kn, vn, ck, cv, fg, fgt, bf_pad, bft, rsum, Bs, L):
    P = ck.shape[1]
    return pl.pallas_call(
        functools.partial(_fox_sample_kernel, L=L),
        grid=(Bs,),
        in_specs=[pl.BlockSpec((L, MIX), lambda b: (b, 0)),
                  pl.BlockSpec((L, MIX), lambda b: (b, 0)),
                  pl.BlockSpec((L, MIX), lambda b: (b, 0)),
                  pl.BlockSpec((1, P, MIX), lambda b: (b, 0, 0)),
                  pl.BlockSpec((1, P, MIX), lambda b: (b, 0, 0)),
                  pl.BlockSpec((L, LANE), lambda b: (b, 0)),
                  pl.BlockSpec((1, LANE, L), lambda b: (b, 0, 0)),
                  pl.BlockSpec((1, LANE), lambda b: (0, 0)),
                  pl.BlockSpec((LANE, 1), lambda b: (0, 0)),
                  pl.BlockSpec((1, N_HEADS, P), lambda b: (b, 0, 0))],
        out_specs=[pl.BlockSpec((L, MIX), lambda b: (b, 0)),
                   pl.BlockSpec((L, LANE), lambda b: (b, 0))],
        out_shape=[jax.ShapeDtypeStruct((Bs * L, MIX), BF16),
                   jax.ShapeDtypeStruct((Bs * L, LANE), F32)],
        compiler_params=_params("parallel"),
        name="fox_sample",
    )(q, kn, vn, ck, cv, fg, fgt, bf_pad, bft, rsum)


def _gdn_prep_kernel(ab_ref, alog_ref, dtb_ref, gc_ref, beta_ref, gr_ref, *, rb, C):
    a = ab_ref[:, :LANE]
    bt = ab_ref[:, LANE:]
    g = -jnp.exp(alog_ref[...]) * _softplus(a + dtb_ref[...])
    beta_ref[...] = _sigmoid(bt)
    r_i = _iota2((rb, rb), 0)
    c_i = _iota2((rb, rb), 1)
    same_chunk_tril = jnp.logical_and(r_i // C == c_i // C, c_i <= r_i).astype(F32)
    G = _dot_hi(same_chunk_tril, g)
    gc_ref[...] = G
    eye = (_iota2((LANE, LANE), 0) == _iota2((LANE, LANE), 1)).astype(F32)
    gr_ref[0, 0] = _dot_nt_hi(eye, G)


def gdn_prep(ab, alog_pad, dtb_pad, B, T, rb, C):
    nb = T // rb
    return pl.pallas_call(
        functools.partial(_gdn_prep_kernel, rb=rb, C=C),
        grid=(B, nb),
        in_specs=[pl.BlockSpec((rb, 2 * LANE), lambda b, i: (b * nb + i, 0)),
                  pl.BlockSpec((1, LANE), lambda b, i: (0, 0)),
                  pl.BlockSpec((1, LANE), lambda b, i: (0, 0))],
        out_specs=[pl.BlockSpec((rb, LANE), lambda b, i: (b * nb + i, 0)),
                   pl.BlockSpec((rb, LANE), lambda b, i: (b * nb + i, 0)),
                   pl.BlockSpec((1, 1, LANE, rb), lambda b, i: (b, i, 0, 0))],
        out_shape=[jax.ShapeDtypeStruct((B * T, LANE), F32),
                   jax.ShapeDtypeStruct((B * T, LANE), F32),
                   jax.ShapeDtypeStruct((B, nb, LANE, rb), F32)],
        compiler_params=_params("parallel", "parallel"),
        name="gdn_prep",
    )(ab, alog_pad, dtb_pad)


def _gdn_kernel(xq_ref, xk_ref, xv_ref, wq_ref, wk_ref, wv_ref, gc_ref, beta_ref, gr_ref, s0_ref,
                bq_ref, bk_ref, bv_ref, gain_ref,
                o_ref, sout_ref, cq_ref, ck_ref, cv_ref, S_ref, tail_ref, *, rb, C, nblk):
    h = pl.program_id(1)
    i = pl.program_id(2)
    nc = rb // C
    n_iter = C.bit_length() - 2

    @pl.when(i == 0)
    def _():
        S_ref[...] = s0_ref[0, 0]
        tail_ref[...] = jnp.zeros_like(tail_ref)
        tail_ref[0, 5:8, :] = bq_ref[0]
        tail_ref[1, 5:8, :] = bk_ref[0]
        tail_ref[2, 5:8, :] = bv_ref[0]

    row8 = _iota2((8, HEAD_DIM), 0)

    def conv_silu(x_ref, w_ref, idx):
        x = x_ref[...]
        w = w_ref[...]
        t8 = tail_ref[idx]
        y = x * w[3:4, :]
        for j in range(1, 4):
            xr = pltpu.roll(x, j, 0)
            head = jnp.where(row8 < j, pltpu.roll(t8, j, 0), xr[:8])
            xs = jnp.concatenate([head, xr[8:]], axis=0)
            y = y + xs * w[3 - j:4 - j, :]
        tail_ref[idx] = x[rb - 8:, :]
        return y * _sigmoid(y)

    qc = conv_silu(xq_ref, wq_ref, 0)
    kc = conv_silu(xk_ref, wk_ref, 1)
    vv = conv_silu(xv_ref, wv_ref, 2)
    qn = qc * lax.rsqrt(jnp.sum(qc * qc, axis=-1, keepdims=True) + EPS) * (HEAD_DIM ** -0.5)
    kn = kc * lax.rsqrt(jnp.sum(kc * kc, axis=-1, keepdims=True) + EPS)

    sel = _iota2((rb, LANE), 1) == h
    g_col = jnp.sum(jnp.where(sel, gc_ref[...], 0.0), axis=1, keepdims=True)
    b_col = jnp.sum(jnp.where(sel, beta_ref[...], 0.0), axis=1, keepdims=True)
    g_row = gr_ref[0, 0, pl.ds(h, 1), :]
    eg_col = jnp.exp(g_col)

    r_i = _iota2((C, C), 0)
    c_i = _iota2((C, C), 1)
    incl = c_i <= r_i
    strict = c_i < r_i
    eye = (c_i == r_i).astype(F32)

    S = S_ref[...]
    outs = []
    for c in range(nc):
        sl = slice(c * C, (c + 1) * C)
        gi = g_col[sl]
        gj = g_row[:, sl]
        bi = b_col[sl]
        egi = eg_col[sl]
        k_c = kn[sl]
        q_c = qn[sl]
        v_c = vv[sl]
        decay = jnp.exp(jnp.where(incl, gi - gj, -jnp.inf))
        A = jnp.where(strict, bi * _dot_nt_hi(k_c, k_c) * decay, 0.0)
        Pm = -A
        Tm = eye + Pm
        for _ in range(n_iter):
            Pm = _dot_hi(Pm, Pm)
            Tm = Tm + _dot_hi(Tm, Pm)
        rhs = jnp.concatenate([v_c * bi, k_c * (bi * egi)], axis=1)
        sol = _dot_hi(Tm, rhs)
        u = sol[:, :HEAD_DIM]
        w = sol[:, HEAD_DIM:]
        qk = jnp.where(incl, _dot_nt_hi(q_c, k_c) * decay, 0.0)
        g_last = gi[C - 1:C, :]
        k_tail = k_c * jnp.exp(g_last - gi)
        q_dec = q_c * egi
        v_new = u - _dot_hi(w, S)
        outs.append(_dot_hi(q_dec, S) + _dot_hi(qk, v_new))
        S = S * jnp.exp(g_last) + lax.dot_general(k_tail, v_new, TN_DIMS, precision=HI,
                                                  preferred_element_type=F32)
    S_ref[...] = S
    o = outs[0] if nc == 1 else jnp.concatenate(outs, axis=0)
    y = o * lax.rsqrt(jnp.mean(o * o, axis=-1, keepdims=True) + EPS) * gain_ref[...]
    o_ref[...] = y.astype(BF16)

    @pl.when(i == nblk - 1)
    def _():
        sout_ref[0, 0] = S
        cq_ref[0] = xq_ref[pl.ds(rb - 3, 3), :]
        ck_ref[0] = xk_ref[pl.ds(rb - 3, 3), :]
        cv_ref[0] = xv_ref[pl.ds(rb - 3, 3), :]


def gdn_mixer(qkv, conv_w, gc, beta, gr, S0, buf0, out_gain, B, T, rb, C):
    nblk = T // rb
    H = N_HEADS
    x_spec = lambda off: pl.BlockSpec((rb, HEAD_DIM), lambda b, h, i: (b * nblk + i, off + h))
    w_spec = lambda off: pl.BlockSpec((4, HEAD_DIM), lambda b, h, i: (0, off + h))
    b_spec = lambda off: pl.BlockSpec((1, 3, HEAD_DIM), lambda b, h, i: (b, 0, off + h))
    col_spec = pl.BlockSpec((rb, LANE), lambda b, h, i: (b * nblk + i, 0))
    c_out = pl.BlockSpec((1, 3, HEAD_DIM), lambda b, h, i: (b, 0, h))
    mix, S, cq, ck, cv = pl.pallas_call(
        functools.partial(_gdn_kernel, rb=rb, C=C, nblk=nblk),
        grid=(B, H, nblk),
        in_specs=[x_spec(0), x_spec(H), x_spec(2 * H), w_spec(0), w_spec(H), w_spec(2 * H),
                  col_spec, col_spec,
                  pl.BlockSpec((1, 1, LANE, rb), lambda b, h, i: (b, i, 0, 0)),
                  pl.BlockSpec((1, 1, HEAD_DIM, HEAD_DIM), lambda b, h, i: (b, h, 0, 0)),
                  b_spec(0), b_spec(H), b_spec(2 * H),
                  pl.BlockSpec((1, HEAD_DIM), lambda b, h, i: (0, 0))],
        out_specs=[pl.BlockSpec((rb, HEAD_DIM), lambda b, h, i: (b * nblk + i, h)),
                   pl.BlockSpec((1, 1, HEAD_DIM, HEAD_DIM), lambda b, h, i: (b, h, 0, 0)),
                   c_out, c_out, c_out],
        out_shape=[jax.ShapeDtypeStruct((B * T, MIX), BF16),
                   jax.ShapeDtypeStruct((B, H, HEAD_DIM, HEAD_DIM), F32),
                   jax.ShapeDtypeStruct((B, 3, MIX), F32),
                   jax.ShapeDtypeStruct((B, 3, MIX), F32),
                   jax.ShapeDtypeStruct((B, 3, MIX), F32)],
        scratch_shapes=[pltpu.VMEM((HEAD_DIM, HEAD_DIM), F32),
                        pltpu.VMEM((3, 8, HEAD_DIM), F32)],
        compiler_params=_params("parallel", "parallel", "arbitrary"),
        name="gdn_mixer",
    )(qkv, qkv, qkv, conv_w, conv_w, conv_w, gc, beta, gr, S0, buf0, buf0, buf0, out_gain.reshape(1, HEAD_DIM))
    return mix, S, jnp.concatenate([cq, ck, cv], axis=-1)


def _pad_cols(w, n):
    return jnp.pad(w, ((0, 0), (0, n - w.shape[1])))


def _pad_vec(v, n):
    return jnp.pad(v.astype(F32), (0, n - v.shape[0])).reshape(1, n)


def kernel(x_prompt, x_sample, mem_prompt, cache_fox_k, cache_fox_v, cache_fox_logf, state_gdn_S, state_gdn_conv, cache_mem_k, cache_mem_v, norm_gain, mem_norm_gain, w_mem_kv, w_in_fox, b_forget, w_in_gdn, gdn_conv_w, gdn_A_log, gdn_dt_bias, gdn_out_norm, w_out, final_norm):
    B, T, D = x_prompt.shape
    Bs, L, _ = x_sample.shape
    depth = norm_gain.shape[0]
    NM = mem_prompt.shape[1]
    P = cache_fox_k.shape[2]
    H = N_HEADS
    scale = HEAD_DIM ** -0.5
    QKV = 3 * MIX

    xp = x_prompt.reshape(B * T, D)
    xs = x_sample.reshape(Bs * L, D)
    mem = mem_prompt.reshape(B * NM, D)

    tq = min(T, 256)
    tk = min(T, 512)
    rb_p = min(T, 4 * CHUNK)

    p_k, p_v, p_lf, p_S, p_conv, p_mk, p_mv = [], [], [], [], [], [], []
    s_k, s_v, s_lf, s_S, s_conv = [], [], [], [], []

    for i in range(depth):
        j = i // 2
        hm = rms_norm_rows(mem, mem_norm_gain[i], BF16)
        wkv = w_mem_kv[i].astype(BF16)
        mk_p = mm(hm, wkv[:, :MEMW], "f32").reshape(B, NM, MEMW)
        mv_p = mm(hm, wkv[:, MEMW:], "f32").reshape(B, NM, MEMW)
        p_mk.append(mk_p)
        p_mv.append(mv_p)

        hp = rms_norm_rows(xp, norm_gain[i], BF16)
        hs = rms_norm_rows(xs, norm_gain[i], BF16)
        w_o = w_out[i].astype(BF16)

        if i % 2 == 0:
            w = w_in_fox[j]
            wq = w[:, :MIX].astype(BF16)
            wk = w[:, MIX:2 * MIX].astype(BF16)
            wv = w[:, 2 * MIX:QKV].astype(BF16)
            wf = _pad_cols(w[:, QKV:QKV + H], LANE).astype(BF16)
            wqm = w[:, QKV + H:QKV + H + MEMW].astype(BF16)
            wz = w[:, QKV + H + MEMW:].astype(BF16)
            bf_pad = _pad_vec(b_forget[j], LANE)

            q = mm(hp, wq, "scale_bf16", scale)
            k32, k16 = mm(hp, wk, "dual")
            v32, v16 = mm(hp, wv, "dual")
            fg = mm(hp, wf, "f32")
            qm_p = mm(hp, wqm, "scale_bf16", scale)
            g_p = mm(hp, wz, "silu_bf16")
            logf, fcol, frow = fox_prep(fg, bf_pad, B, T, tk)
            mix_p = fox_attn(q, k16, v16, fcol, frow, B, T, tq, tk)
            p_k.append(k32.reshape(B, T, H, HEAD_DIM))
            p_v.append(v32.reshape(B, T, H, HEAD_DIM))
            p_lf.append(logf[:, :H].reshape(B, T, H))

            q = mm(hs, wq, "scale_bf16", scale)
            k32, k16 = mm(hs, wk, "dual")
            v32, v16 = mm(hs, wv, "dual")
            fg = mm(hs, wf, "f32")
            qm_s = mm(hs, wqm, "scale_bf16", scale)
            g_s = mm(hs, wz, "silu_bf16")
            fgt = fg.reshape(Bs, L, LANE).transpose(0, 2, 1)
            clf_t = cache_fox_logf[j].astype(F32).transpose(0, 2, 1).reshape(Bs * H, P)
            rsum = suffix_sum(clf_t).reshape(Bs, H, P)
            mix_s, logf_s = fox_sample(q, k16, v16,
                                       cache_fox_k[j].reshape(Bs, P, MIX), cache_fox_v[j].reshape(Bs, P, MIX),
                                       fg, fgt, bf_pad, bf_pad.reshape(LANE, 1), rsum, Bs, L)
            s_k.append(k32.reshape(Bs, L, H, HEAD_DIM))
            s_v.append(v32.reshape(Bs, L, H, HEAD_DIM))
            s_lf.append(logf_s[:, :H].reshape(Bs, L, H))
        else:
            w = w_in_gdn[j]
            wqkv = w[:, :QKV].astype(BF16)
            wab = jnp.concatenate([_pad_cols(w[:, QKV:QKV + H], LANE),
                                   _pad_cols(w[:, QKV + H:QKV + 2 * H], LANE)], axis=1).astype(BF16)
            wqm = w[:, QKV + 2 * H:QKV + 2 * H + MEMW].astype(BF16)
            wz = w[:, QKV + 2 * H + MEMW:].astype(BF16)
            alog_pad = _pad_vec(gdn_A_log[j], LANE)
            dtb_pad = _pad_vec(gdn_dt_bias[j], LANE)

            qkv = mm(hp, wqkv, "f32")
            ab = mm(hp, wab, "f32")
            qm_p = mm(hp, wqm, "scale_bf16", scale)
            g_p = mm(hp, wz, "silu_bf16")
            gc, beta, gr = gdn_prep(ab, alog_pad, dtb_pad, B, T, rb_p, min(CHUNK, T))
            mix_p, S, conv = gdn_mixer(qkv, gdn_conv_w[j], gc, beta, gr,
                                       jnp.zeros((B, H, HEAD_DIM, HEAD_DIM), F32),
                                       jnp.zeros((B, 3, QKV), F32),
                                       gdn_out_norm[j], B, T, rb_p, min(CHUNK, T))
            p_S.append(S)
            p_conv.append(conv)

            qkv = mm(hs, wqkv, "f32")
            ab = mm(hs, wab, "f32")
            qm_s = mm(hs, wqm, "scale_bf16", scale)
            g_s = mm(hs, wz, "silu_bf16")
            gc, beta, gr = gdn_prep(ab, alog_pad, dtb_pad, Bs, L, L, L)
            mix_s, S, conv = gdn_mixer(qkv, gdn_conv_w[j], gc, beta, gr,
                                       state_gdn_S[j].astype(F32), state_gdn_conv[j].astype(F32),
                                       gdn_out_norm[j], Bs, L, L, L)
            s_S.append(S)
            s_conv.append(conv)

        m_p = mem_attn(qm_p, mk_p, mv_p, T)
        m_s = mem_attn(qm_s, cache_mem_k[i].reshape(Bs, NM, MEMW), cache_mem_v[i].reshape(Bs, NM, MEMW), L)
        xp = out_proj(mix_p, m_p, g_p, w_o, xp)
        xs = out_proj(mix_s, m_s, g_s, w_o, xs)

    y_prompt = rms_norm_rows(xp, final_norm, F32).reshape(B, T, D)
    y_sample = rms_norm_rows(xs, final_norm, F32).reshape(Bs, L, D)
    mem_shape = (depth, B, NM, N_MEM_HEADS, HEAD_DIM)
    return (y_prompt, y_sample,
            jnp.stack(p_k), jnp.stack(p_v), jnp.stack(p_lf), jnp.stack(p_S), jnp.stack(p_conv),
            jnp.stack(p_mk).reshape(mem_shape), jnp.stack(p_mv).reshape(mem_shape),
            jnp.stack(s_k), jnp.stack(s_v), jnp.stack(s_lf), jnp.stack(s_S), jnp.stack(s_conv))
```

```python
import functools
import math

import jax
import jax.numpy as jnp
from jax import lax
from jax.experimental import pallas as pl
from jax.experimental.pallas import tpu as pltpu

F32 = jnp.float32
BF16 = jnp.bfloat16
HI = lax.Precision.HIGHEST

LANE = 128
HEAD_DIM = 128
N_HEADS = 12
MIX = N_HEADS * HEAD_DIM
N_MEM_HEADS = 4
MEMW = N_MEM_HEADS * HEAD_DIM
CHUNK = 64
EPS = 1e-6
LOG2E = math.log2(math.e)
VMEM_LIMIT = 48 * 1024 * 1024
VMEM_LIMIT_BIG = 58 * 1024 * 1024

NN_DIMS = (((1,), (0,)), ((), ()))
NT_DIMS = (((1,), (1,)), ((), ()))


def _params(*sem, vmem=VMEM_LIMIT):
    return pltpu.CompilerParams(dimension_semantics=sem, vmem_limit_bytes=vmem)


def _sigmoid(x):
    return 1.0 / (1.0 + jnp.exp(-x))


def _log_sigmoid(x):
    return jnp.minimum(x, 0.0) - jnp.log1p(jnp.exp(-jnp.abs(x)))


def _softplus(x):
    return jnp.maximum(x, 0.0) + jnp.log1p(jnp.exp(-jnp.abs(x)))


def _dot_hi(a, b):
    return jnp.dot(a, b, precision=HI, preferred_element_type=F32)


def _dot_nt_hi(a, b):
    return lax.dot_general(a, b, NT_DIMS, precision=HI, preferred_element_type=F32)


def _split_bf16(a):
    hi = a.astype(BF16)
    lo = (a - hi.astype(F32)).astype(BF16)
    return hi, lo


def _mm1(a, b, dims=NN_DIMS):
    return lax.dot_general(a.astype(BF16), b.astype(BF16), dims, preferred_element_type=F32)


def _mm2r(a, b):
    b_hi, b_lo = _split_bf16(b)
    aa = a.astype(BF16)
    return (jnp.dot(aa, b_hi, preferred_element_type=F32)
            + jnp.dot(aa, b_lo, preferred_element_type=F32))


def _iota2(shape, dim):
    return lax.broadcasted_iota(jnp.int32, shape, dim)


def _rms_kernel(x_ref, g_ref, o_ref):
    x = x_ref[...]
    y = x * lax.rsqrt(jnp.mean(x * x, axis=-1, keepdims=True) + EPS)
    o_ref[...] = (y * g_ref[...]).astype(o_ref.dtype)


def rms_norm_rows(x, gain, out_dtype):
    R, D = x.shape
    tm = min(R, 512)
    return pl.pallas_call(
        _rms_kernel,
        grid=(R // tm,),
        in_specs=[pl.BlockSpec((tm, D), lambda i: (i, 0)),
                  pl.BlockSpec((1, D), lambda i: (0, 0))],
        out_specs=pl.BlockSpec((tm, D), lambda i: (i, 0)),
        out_shape=jax.ShapeDtypeStruct((R, D), out_dtype),
        compiler_params=_params("parallel"),
        name="rms_norm",
    )(x, gain.reshape(1, D))


def _mm_kernel(h_ref, w_ref, *o_refs, kind, scale):
    acc = jnp.dot(h_ref[...], w_ref[...], preferred_element_type=F32)
    if kind == "f32":
        o_refs[0][...] = acc
    elif kind == "scale_bf16":
        o_refs[0][...] = (acc * scale).astype(BF16)
    elif kind == "dual":
        o_refs[0][...] = acc
        o_refs[1][...] = acc.astype(BF16)
    elif kind == "silu_bf16":
        o_refs[0][...] = (acc * _sigmoid(acc)).astype(BF16)
    else:
        raise ValueError(kind)


def mm(h, w, kind, scale=1.0):
    R, K = h.shape
    N = w.shape[1]
    tm = min(R, 1024)
    tn = min(N, 512)
    out_dtypes = {"f32": [F32], "scale_bf16": [BF16], "dual": [F32, BF16], "silu_bf16": [BF16]}[kind]
    outs = pl.pallas_call(
        functools.partial(_mm_kernel, kind=kind, scale=scale),
        grid=(R // tm, N // tn),
        in_specs=[pl.BlockSpec((tm, K), lambda i, n: (i, 0)),
                  pl.BlockSpec((K, tn), lambda i, n: (0, n))],
        out_specs=[pl.BlockSpec((tm, tn), lambda i, n: (i, n)) for _ in out_dtypes],
        out_shape=[jax.ShapeDtypeStruct((R, N), dt) for dt in out_dtypes],
        compiler_params=_params("parallel", "arbitrary"),
        name="proj_" + kind,
    )(h, w)
    return outs if len(outs) > 1 else outs[0]


def _out_proj_kernel(mix_ref, m_ref, g_ref, w_ref, x_ref, o_ref, br_ref):
    @pl.when(pl.program_id(1) == 0)
    def _():
        g = g_ref[...].astype(F32)
        br_ref[:, :MIX] = (mix_ref[...].astype(F32) * g[:, :MIX]).astype(BF16)
        br_ref[:, MIX:] = (m_ref[...].astype(F32) * g[:, MIX:]).astype(BF16)

    o_ref[...] = x_ref[...] + jnp.dot(br_ref[...], w_ref[...], preferred_element_type=F32)


def out_proj(mix, m, g, w, x):
    R, D = x.shape
    BW = MIX + MEMW
    tm = min(R, 1024)
    tn = min(D, 512)
    return pl.pallas_call(
        _out_proj_kernel,
        grid=(R // tm, D // tn),
        in_specs=[pl.BlockSpec((tm, MIX), lambda i, n: (i, 0)),
                  pl.BlockSpec((tm, MEMW), lambda i, n: (i, 0)),
                  pl.BlockSpec((tm, BW), lambda i, n: (i, 0)),
                  pl.BlockSpec((BW, tn), lambda i, n: (0, n)),
                  pl.BlockSpec((tm, tn), lambda i, n: (i, n))],
        out_specs=pl.BlockSpec((tm, tn), lambda i, n: (i, n)),
        out_shape=jax.ShapeDtypeStruct((R, D), F32),
        scratch_shapes=[pltpu.VMEM((tm, BW), BF16)],
        compiler_params=_params("parallel", "arbitrary"),
        name="out_proj",
    )(mix, m, g, w, x)


def _mem_attn_kernel(q_ref, k_ref, v_ref, o_ref, *, per_head_kv):
    for h in range(N_MEM_HEADS):
        sl = slice(h * HEAD_DIM, (h + 1) * HEAD_DIM)
        q = q_ref[:, sl]
        if per_head_kv:
            k = k_ref[0, :, h, :].astype(BF16)
            v = v_ref[0, :, h, :].astype(BF16)
        else:
            k = k_ref[0, :, sl].astype(BF16)
            v = v_ref[0, :, sl].astype(BF16)
        s = lax.dot_general(q, k, NT_DIMS, preferred_element_type=F32)
        m = jnp.max(s, axis=1, keepdims=True)
        p = jnp.exp(s - m)
        l = jnp.sum(p, axis=1, keepdims=True)
        o = jnp.dot(p.astype(BF16), v, preferred_element_type=F32) / l
        o_ref[:, sl] = o.astype(BF16)


def mem_attn(q, mk, mv, rows_per_batch):
    R = q.shape[0]
    NB, NM = mk.shape[:2]
    per_head_kv = mk.ndim == 4
    tr = min(rows_per_batch, 1024)
    nrb = rows_per_batch // tr
    if per_head_kv:
        kv_spec = pl.BlockSpec((1, NM, N_MEM_HEADS, HEAD_DIM), lambda b, i: (b, 0, 0, 0))
    else:
        kv_spec = pl.BlockSpec((1, NM, MEMW), lambda b, i: (b, 0, 0))
    return pl.pallas_call(
        functools.partial(_mem_attn_kernel, per_head_kv=per_head_kv),
        grid=(NB, nrb),
        in_specs=[pl.BlockSpec((tr, MEMW), lambda b, i: (b * nrb + i, 0)), kv_spec, kv_spec],
        out_specs=pl.BlockSpec((tr, MEMW), lambda b, i: (b * nrb + i, 0)),
        out_shape=jax.ShapeDtypeStruct((R, MEMW), BF16),
        compiler_params=_params("parallel", "parallel"),
        name="mem_attn",
    )(q, mk, mv)


def _fox_prep_kernel(fg_ref, bf_ref, logf_ref, qa_ref, ka_ref, carry_ref, *, tp):
    @pl.when(pl.program_id(1) == 0)
    def _():
        carry_ref[...] = jnp.zeros_like(carry_ref)

    logf = _log_sigmoid(fg_ref[...] + bf_ref[...])
    tril = (_iota2((tp, tp), 1) <= _iota2((tp, tp), 0)).astype(F32)
    F = _dot_hi(tril, logf) + carry_ref[...]
    carry_ref[...] = F[tp - 1:tp, :]
    logf_ref[...] = logf

    F2 = F * LOG2E
    hi = F2.astype(BF16)
    r1 = F2 - hi.astype(F32)
    mid = r1.astype(BF16)
    lo = (r1 - mid.astype(F32)).astype(BF16)
    col = _iota2((LANE, MIX), 1)
    own_head = _iota2((LANE, MIX), 0) == (col >> 7)
    slot = col & (LANE - 1)

    def spread(x, c):
        sel = jnp.logical_and(own_head, slot == c).astype(BF16)
        return jnp.dot(x, sel, preferred_element_type=F32)

    slot_row = _iota2((1, MIX), 1) & (LANE - 1)
    ones_q = jnp.logical_and(slot_row >= 3, slot_row < 6).astype(F32)
    ones_k = (slot_row < 3).astype(F32)
    qa_ref[...] = (spread(hi, 0) + spread(mid, 1) + spread(lo, 2) + ones_q).astype(BF16)
    ka_ref[...] = (ones_k - (spread(hi, 3) + spread(mid, 4) + spread(lo, 5))).astype(BF16)


def fox_prep(fg, bf_pad, B, T, tp):
    nb = T // tp
    return pl.pallas_call(
        functools.partial(_fox_prep_kernel, tp=tp),
        grid=(B, nb),
        in_specs=[pl.BlockSpec((tp, LANE), lambda b, i: (b * nb + i, 0)),
                  pl.BlockSpec((1, LANE), lambda b, i: (0, 0))],
        out_specs=[pl.BlockSpec((tp, LANE), lambda b, i: (b * nb + i, 0)),
                   pl.BlockSpec((tp, MIX), lambda b, i: (b * nb + i, 0)),
                   pl.BlockSpec((tp, MIX), lambda b, i: (b * nb + i, 0))],
        out_shape=[jax.ShapeDtypeStruct((B * T, LANE), F32),
                   jax.ShapeDtypeStruct((B * T, MIX), BF16),
                   jax.ShapeDtypeStruct((B * T, MIX), BF16)],
        scratch_shapes=[pltpu.VMEM((1, LANE), F32)],
        compiler_params=_params("parallel", "arbitrary"),
        name="fox_prep",
    )(fg, bf_pad)


def _fox_attn_kernel(q_ref, qa_ref, k_ref, ka_ref, v_ref, o_ref, *, tq, tk, n_sub):
    qi = pl.program_id(2)
    sub = tq // n_sub
    qc = jnp.concatenate([q_ref[...], qa_ref[...]], axis=1)
    col = _iota2((sub, tk), 1)
    rows = [qi * tq + s * sub + _iota2((sub, tk), 0) for s in range(n_sub)]

    def body(kj, stats, masked):
        off = pl.multiple_of(kj * tk, tk)
        kc = jnp.concatenate([k_ref[pl.ds(off, tk), :], ka_ref[pl.ds(off, tk), :]], axis=1)
        vb = v_ref[pl.ds(off, tk), :]
        sc_all = lax.dot_general(qc, kc, NT_DIMS, preferred_element_type=F32)
        new = []
        for s in range(n_sub):
            m, l, acc = stats[s]
            sc = sc_all[s * sub:(s + 1) * sub, :]
            if masked:
                sc = jnp.where(col + kj * tk <= rows[s], sc, -jnp.inf)
            m_new = jnp.maximum(m, jnp.max(sc, axis=1, keepdims=True))
            alpha = jnp.exp2(m - m_new)
            p = jnp.exp2(sc - m_new)
            l = alpha * l + jnp.sum(p, axis=1, keepdims=True)
            acc = alpha * acc + jnp.dot(p.astype(BF16), vb, preferred_element_type=F32)
            new.append((m_new, l, acc))
        return tuple(new)

    n_full = (qi * tq + 1) // tk
    n_all = ((qi + 1) * tq + tk - 1) // tk
    stats = tuple((jnp.full((sub, 1), -jnp.inf, F32), jnp.zeros((sub, 1), F32), jnp.zeros((sub, HEAD_DIM), F32))
                  for _ in range(n_sub))
    stats = lax.fori_loop(0, n_full, functools.partial(body, masked=False), stats)
    stats = lax.fori_loop(n_full, n_all, functools.partial(body, masked=True), stats)
    for s in range(n_sub):
        m, l, acc = stats[s]
        o_ref[s * sub:(s + 1) * sub, :] = (acc / l).astype(BF16)


def fox_attn(q, qa, k, ka, v, B, T, tq, tk):
    nq = T // tq
    n_sub = 2 if tq >= 512 else 1
    q_spec = pl.BlockSpec((tq, HEAD_DIM), lambda b, h, i: (b * nq + i, h))
    kv_spec = pl.BlockSpec((T, HEAD_DIM), lambda b, h, i: (b, h))
    return pl.pallas_call(
        functools.partial(_fox_attn_kernel, tq=tq, tk=tk, n_sub=n_sub),
        grid=(B, N_HEADS, nq),
        in_specs=[q_spec, q_spec, kv_spec, kv_spec, kv_spec],
        out_specs=pl.BlockSpec((tq, HEAD_DIM), lambda b, h, i: (b * nq + i, h)),
        out_shape=jax.ShapeDtypeStruct((B * T, MIX), BF16),
        compiler_params=_params("parallel", "parallel", "parallel"),
        name="fox_attn",
    )(q, qa, k, ka, v)


def _suffix_sum_kernel(x_ref, o_ref):
    P = x_ref.shape[1]
    after = (_iota2((P, P), 0) > _iota2((P, P), 1)).astype(F32)
    o_ref[...] = _dot_hi(x_ref[...], after)


def suffix_sum(x):
    return pl.pallas_call(
        _suffix_sum_kernel,
        out_shape=jax.ShapeDtypeStruct(x.shape, F32),
        compiler_params=pltpu.CompilerParams(vmem_limit_bytes=VMEM_LIMIT),
        name="fox_suffix_sum",
    )(x)


def _fox_sample_kernel(q_ref, kn_ref, vn_ref, ck_ref, cv_ref, fg_ref, fgt_ref, bf_ref, bft_ref, r_ref,
                       o_ref, logf_ref, *, L):
    logf = _log_sigmoid(fg_ref[...] + bf_ref[...])
    logf_ref[...] = logf
    r_i = _iota2((L, L), 0)
    c_i = _iota2((L, L), 1)
    causal = c_i <= r_i
    ln_col = _dot_hi(causal.astype(F32), logf)
    logf_t = _log_sigmoid(fgt_ref[0] + bft_ref[...])
    ln_row = _dot_hi(logf_t, (r_i <= c_i).astype(F32))
    for h in range(N_HEADS):
        sl = slice(h * HEAD_DIM, (h + 1) * HEAD_DIM)
        q = q_ref[:, sl]
        kc = ck_ref[0, :, h, :].astype(BF16)
        vc = cv_ref[0, :, h, :].astype(BF16)
        fq = ln_col[:, h:h + 1]
        sc = lax.dot_general(q, kc, NT_DIMS, preferred_element_type=F32) + (fq + r_ref[0, h:h + 1, :])
        sn = lax.dot_general(q, kn_ref[:, sl], NT_DIMS, preferred_element_type=F32) + (fq - ln_row[h:h + 1, :])
        sn = jnp.where(causal, sn, -jnp.inf)
        m = jnp.maximum(jnp.max(sc, axis=1, keepdims=True), jnp.max(sn, axis=1, keepdims=True))
        pc = jnp.exp(sc - m)
        pn = jnp.exp(sn - m)
        l = jnp.sum(pc, axis=1, keepdims=True) + jnp.sum(pn, axis=1, keepdims=True)
        o = (jnp.dot(pc.astype(BF16), vc, preferred_element_type=F32)
             + jnp.dot(pn.astype(BF16), vn_ref[:, sl], preferred_element_type=F32)) / l
        o_ref[:, sl] = o.astype(BF16)


def fox_sample(q, kn, vn, ck, cv, fg, fgt, bf_pad, bft, rsum, Bs, L):
    P = ck.shape[1]
    cache_spec = pl.BlockSpec((1, P, N_HEADS, HEAD_DIM), lambda b: (b, 0, 0, 0))
    return pl.pallas_call(
        functools.partial(_fox_sample_kernel, L=L),
        grid=(Bs,),
        in_specs=[pl.BlockSpec((L, MIX), lambda b: (b, 0)),
                  pl.BlockSpec((L, MIX), lambda b: (b, 0)),
                  pl.BlockSpec((L, MIX), lambda b: (b, 0)),
                  cache_spec, cache_spec,
                  pl.BlockSpec((L, LANE), lambda b: (b, 0)),
                  pl.BlockSpec((1, LANE, L), lambda b: (b, 0, 0)),
                  pl.BlockSpec((1, LANE), lambda b: (0, 0)),
                  pl.BlockSpec((LANE, 1), lambda b: (0, 0)),
                  pl.BlockSpec((1, N_HEADS, P), lambda b: (b, 0, 0))],
        out_specs=[pl.BlockSpec((L, MIX), lambda b: (b, 0)),
                   pl.BlockSpec((L, LANE), lambda b: (b, 0))],
        out_shape=[jax.ShapeDtypeStruct((Bs * L, MIX), BF16),
                   jax.ShapeDtypeStruct((Bs * L, LANE), F32)],
        compiler_params=_params("parallel", vmem=VMEM_LIMIT_BIG),
        name="fox_sample",
    )(q, kn, vn, ck, cv, fg, fgt, bf_pad, bft, rsum)


def _gdn_prep_kernel(ab_ref, alog_ref, dtb_ref, gc_ref, beta_ref, gr_ref, *, rb, C):
    a = ab_ref[:, :LANE]
    bt = ab_ref[:, LANE:]
    g = -jnp.exp(alog_ref[...]) * _softplus(a + dtb_ref[...])
    beta_ref[...] = _sigmoid(bt)
    r_i = _iota2((rb, rb), 0)
    c_i = _iota2((rb, rb), 1)
    same_chunk_tril = jnp.logical_and(r_i // C == c_i // C, c_i <= r_i).astype(F32)
    G = _dot_hi(same_chunk_tril, g)
    gc_ref[...] = G
    eye = (_iota2((LANE, LANE), 0) == _iota2((LANE, LANE), 1)).astype(F32)
    gr_ref[0, 0] = _dot_nt_hi(eye, G)


def gdn_prep(ab, alog_pad, dtb_pad, B, T, rb, C):
    nb = T // rb
    return pl.pallas_call(
        functools.partial(_gdn_prep_kernel, rb=rb, C=C),
        grid=(B, nb),
        in_specs=[pl.BlockSpec((rb, 2 * LANE), lambda b, i: (b * nb + i, 0)),
                  pl.BlockSpec((1, LANE), lambda b, i: (0, 0)),
                  pl.BlockSpec((1, LANE), lambda b, i: (0, 0))],
        out_specs=[pl.BlockSpec((rb, LANE), lambda b, i: (b * nb + i, 0)),
                   pl.BlockSpec((rb, LANE), lambda b, i: (b * nb + i, 0)),
                   pl.BlockSpec((1, 1, LANE, rb), lambda b, i: (b, i, 0, 0))],
        out_shape=[jax.ShapeDtypeStruct((B * T, LANE), F32),
                   jax.ShapeDtypeStruct((B * T, LANE), F32),
                   jax.ShapeDtypeStruct((B, nb, LANE, rb), F32)],
        compiler_params=_params("parallel", "parallel"),
        name="gdn_prep",
    )(ab, alog_pad, dtb_pad)


def _gdn_kernel(xq_ref, xk_ref, xv_ref, wq_ref, wk_ref, wv_ref, gc_ref, beta_ref, gr_ref, s0_ref,
                bq_ref, bk_ref, bv_ref, gain_ref,
                o_ref, sout_ref, cq_ref, ck_ref, cv_ref, S_ref, tail_ref, *, rb, C, nblk, hb):
    hblk = pl.program_id(1)
    i = pl.program_id(2)
    nc = rb // C
    W = hb * HEAD_DIM
    n_iter = C.bit_length() - 2

    @pl.when(i == 0)
    def _():
        S_ref[...] = s0_ref[0]
        tail_ref[...] = jnp.zeros_like(tail_ref)
        tail_ref[0, 5:8, :] = bq_ref[0]
        tail_ref[1, 5:8, :] = bk_ref[0]
        tail_ref[2, 5:8, :] = bv_ref[0]

    row8 = _iota2((8, W), 0)

    def conv_silu(x_ref, w_ref, idx):
        x = x_ref[...]
        w = w_ref[...]
        t8 = tail_ref[idx]
        y = x * w[3:4, :]
        for j in range(1, 4):
            xr = pltpu.roll(x, j, 0)
            head = jnp.where(row8 < j, pltpu.roll(t8, j, 0), xr[:8])
            xs = jnp.concatenate([head, xr[8:]], axis=0)
            y = y + xs * w[3 - j:4 - j, :]
        tail_ref[idx] = x[rb - 8:, :]
        return y * _sigmoid(y)

    qc_all = conv_silu(xq_ref, wq_ref, 0)
    kc_all = conv_silu(xk_ref, wk_ref, 1)
    vv_all = conv_silu(xv_ref, wv_ref, 2)

    r_i = _iota2((C, C), 0)
    c_i = _iota2((C, C), 1)
    incl = c_i <= r_i
    strict = c_i < r_i
    lane = _iota2((rb, LANE), 1)
    gc_blk = gc_ref[...]
    beta_blk = beta_ref[...]

    heads = []
    for hh in range(hb):
        hsl = slice(hh * HEAD_DIM, (hh + 1) * HEAD_DIM)
        qc = qc_all[:, hsl]
        kc = kc_all[:, hsl]
        qn = qc * lax.rsqrt(jnp.sum(qc * qc, axis=-1, keepdims=True) + EPS) * (HEAD_DIM ** -0.5)
        kn = kc * lax.rsqrt(jnp.sum(kc * kc, axis=-1, keepdims=True) + EPS)
        head_id = hblk * hb + hh
        sel = lane == head_id
        g_col = jnp.sum(jnp.where(sel, gc_blk, 0.0), axis=1, keepdims=True)
        b_col = jnp.sum(jnp.where(sel, beta_blk, 0.0), axis=1, keepdims=True)
        g_row = gr_ref[0, 0, pl.ds(head_id, 1), :]
        heads.append(dict(qn=qn, kn=kn, vv=vv_all[:, hsl], g_col=g_col, b_col=b_col, g_row=g_row,
                          eg_col=jnp.exp(g_col), S=S_ref[hh], outs=[]))

    units = []
    for c in range(nc):
        sl = slice(c * C, (c + 1) * C)
        for hd in heads:
            gi = hd["g_col"][sl]
            u_ = dict(hd=hd, gi=gi, bi=hd["b_col"][sl], egi=hd["eg_col"][sl],
                      k=hd["kn"][sl], q=hd["qn"][sl], v=hd["vv"][sl])
            u_["decay"] = jnp.exp(jnp.where(incl, gi - hd["g_row"][:, sl], -jnp.inf))
            units.append(u_)
    for u_ in units:
        u_["kq"] = _mm1(jnp.concatenate([u_["k"], u_["q"]], axis=0), u_["k"], NT_DIMS)
    for u_ in units:
        u_["P"] = -jnp.where(strict, u_["bi"] * u_["kq"][:C] * u_["decay"], 0.0)
        u_["qk"] = jnp.where(incl, u_["kq"][C:] * u_["decay"], 0.0)
        u_["X"] = jnp.concatenate([u_["v"] * u_["bi"], u_["k"] * (u_["bi"] * u_["egi"])], axis=1)
    for _ in range(n_iter):
        for u_ in units:
            u_["Y"] = _mm2r(u_["P"], jnp.concatenate([u_["X"], u_["P"]], axis=1))
        for u_ in units:
            u_["X"] = u_["X"] + u_["Y"][:, :2 * HEAD_DIM]
            u_["P"] = u_["Y"][:, 2 * HEAD_DIM:]
    for u_ in units:
        u_["Y"] = _mm2r(u_["P"], u_["X"])
    for u_ in units:
        X = u_["X"] + u_["Y"]
        g_last = u_["gi"][C - 1:C, :]
        u_["u"] = X[:, :HEAD_DIM]
        u_["wq"] = jnp.concatenate([X[:, HEAD_DIM:], u_["q"] * u_["egi"]], axis=0)
        u_["qk_kt"] = jnp.concatenate([u_["qk"], (u_["k"] * jnp.exp(g_last - u_["gi"])).T], axis=0)
        u_["g_last"] = jnp.exp(g_last)

    for c in range(nc):
        cu = units[c * hb:(c + 1) * hb]
        for u_ in cu:
            u_["ws_qs"] = _mm1(u_["wq"], u_["hd"]["S"])
        for u_ in cu:
            u_["v_new"] = u_["u"] - u_["ws_qs"][:C]
            u_["od"] = _mm2r(u_["qk_kt"], u_["v_new"])
        for u_ in cu:
            hd = u_["hd"]
            hd["outs"].append(u_["ws_qs"][C:] + u_["od"][:C])
            hd["S"] = hd["S"] * u_["g_last"] + u_["od"][C:]

    gain = gain_ref[...]
    for hh, hd in enumerate(heads):
        hsl = slice(hh * HEAD_DIM, (hh + 1) * HEAD_DIM)
        S_ref[hh] = hd["S"]
        o = hd["outs"][0] if nc == 1 else jnp.concatenate(hd["outs"], axis=0)
        y = o * lax.rsqrt(jnp.mean(o * o, axis=-1, keepdims=True) + EPS) * gain
        o_ref[:, hsl] = y.astype(BF16)

    @pl.when(i == nblk - 1)
    def _():
        for hh, hd in enumerate(heads):
            sout_ref[0, hh] = hd["S"]
        cq_ref[0] = xq_ref[pl.ds(rb - 3, 3), :]
        ck_ref[0] = xk_ref[pl.ds(rb - 3, 3), :]
        cv_ref[0] = xv_ref[pl.ds(rb - 3, 3), :]


def gdn_mixer(qkv, conv_w, gc, beta, gr, S0, buf0, out_gain, B, T, rb, C, hb):
    nblk = T // rb
    H = N_HEADS
    W = hb * HEAD_DIM
    nhb = H // hb
    x_spec = lambda off: pl.BlockSpec((rb, W), lambda b, h, i: (b * nblk + i, off + h))
    w_spec = lambda off: pl.BlockSpec((4, W), lambda b, h, i: (0, off + h))
    b_spec = lambda off: pl.BlockSpec((1, 3, W), lambda b, h, i: (b, 0, off + h))
    col_spec = pl.BlockSpec((rb, LANE), lambda b, h, i: (b * nblk + i, 0))
    s_spec = pl.BlockSpec((1, hb, HEAD_DIM, HEAD_DIM), lambda b, h, i: (b, h, 0, 0))
    c_out = pl.BlockSpec((1, 3, W), lambda b, h, i: (b, 0, h))
    mix, S, cq, ck, cv = pl.pallas_call(
        functools.partial(_gdn_kernel, rb=rb, C=C, nblk=nblk, hb=hb),
        grid=(B, nhb, nblk),
        in_specs=[x_spec(0), x_spec(nhb), x_spec(2 * nhb), w_spec(0), w_spec(nhb), w_spec(2 * nhb),
                  col_spec, col_spec,
                  pl.BlockSpec((1, 1, LANE, rb), lambda b, h, i: (b, i, 0, 0)),
                  s_spec,
                  b_spec(0), b_spec(nhb), b_spec(2 * nhb),
                  pl.BlockSpec((1, HEAD_DIM), lambda b, h, i: (0, 0))],
        out_specs=[pl.BlockSpec((rb, W), lambda b, h, i: (b * nblk + i, h)),
                   s_spec, c_out, c_out, c_out],
        out_shape=[jax.ShapeDtypeStruct((B * T, MIX), BF16),
                   jax.ShapeDtypeStruct((B, H, HEAD_DIM, HEAD_DIM), F32),
                   jax.ShapeDtypeStruct((B, 3, MIX), F32),
                   jax.ShapeDtypeStruct((B, 3, MIX), F32),
                   jax.ShapeDtypeStruct((B, 3, MIX), F32)],
        scratch_shapes=[pltpu.VMEM((hb, HEAD_DIM, HEAD_DIM), F32),
                        pltpu.VMEM((3, 8, W), F32)],
        compiler_params=_params("parallel", "parallel", "arbitrary"),
        name="gdn_mixer",
    )(qkv, qkv, qkv, conv_w, conv_w, conv_w, gc, beta, gr, S0, buf0, buf0, buf0, out_gain.reshape(1, HEAD_DIM))
    return mix, S, jnp.concatenate([cq, ck, cv], axis=-1)


def _pad_cols(w, n):
    return jnp.pad(w, ((0, 0), (0, n - w.shape[1])))


def _pad_vec(v, n):
    return jnp.pad(v.astype(F32), (0, n - v.shape[0])).reshape(1, n)


def kernel(x_prompt, x_sample, mem_prompt, cache_fox_k, cache_fox_v, cache_fox_logf, state_gdn_S, state_gdn_conv, cache_mem_k, cache_mem_v, norm_gain, mem_norm_gain, w_mem_kv, w_in_fox, b_forget, w_in_gdn, gdn_conv_w, gdn_A_log, gdn_dt_bias, gdn_out_norm, w_out, final_norm):
    B, T, D = x_prompt.shape
    Bs, L, _ = x_sample.shape
    depth = norm_gain.shape[0]
    NM = mem_prompt.shape[1]
    P = cache_fox_k.shape[2]
    H = N_HEADS
    scale = HEAD_DIM ** -0.5
    QKV = 3 * MIX

    xp = x_prompt.reshape(B * T, D)
    xs = x_sample.reshape(Bs * L, D)
    mem = mem_prompt.reshape(B * NM, D)

    tq = min(T, 512)
    tk = min(T, 512)
    rb_p = min(T, 4 * CHUNK)

    p_k, p_v, p_lf, p_S, p_conv, p_mk, p_mv = [], [], [], [], [], [], []
    s_k, s_v, s_lf, s_S, s_conv = [], [], [], [], []

    for i in range(depth):
        j = i // 2
        hm = rms_norm_rows(mem, mem_norm_gain[i], BF16)
        wkv = w_mem_kv[i].astype(BF16)
        mk_p = mm(hm, wkv[:, :MEMW], "f32").reshape(B, NM, MEMW)
        mv_p = mm(hm, wkv[:, MEMW:], "f32").reshape(B, NM, MEMW)
        p_mk.append(mk_p)
        p_mv.append(mv_p)

        hp = rms_norm_rows(xp, norm_gain[i], BF16)
        hs = rms_norm_rows(xs, norm_gain[i], BF16)
        w_o = w_out[i].astype(BF16)

        if i % 2 == 0:
            w = w_in_fox[j]
            wq = w[:, :MIX].astype(BF16)
            wk = w[:, MIX:2 * MIX].astype(BF16)
            wv = w[:, 2 * MIX:QKV].astype(BF16)
            wf = _pad_cols(w[:, QKV:QKV + H], LANE).astype(BF16)
            wqm = w[:, QKV + H:QKV + H + MEMW].astype(BF16)
            wz = w[:, QKV + H + MEMW:].astype(BF16)
            bf_pad = _pad_vec(b_forget[j], LANE)

            q = mm(hp, wq, "scale_bf16", scale * LOG2E)
            k32, k16 = mm(hp, wk, "dual")
            v32, v16 = mm(hp, wv, "dual")
            fg = mm(hp, wf, "f32")
            qm_p = mm(hp, wqm, "scale_bf16", scale)
            g_p = mm(hp, wz, "silu_bf16")
            logf, qa, ka = fox_prep(fg, bf_pad, B, T, tk)
            mix_p = fox_attn(q, qa, k16, ka, v16, B, T, tq, tk)
            p_k.append(k32.reshape(B, T, H, HEAD_DIM))
            p_v.append(v32.reshape(B, T, H, HEAD_DIM))
            p_lf.append(logf[:, :H].reshape(B, T, H))

            q = mm(hs, wq, "scale_bf16", scale)
            k32, k16 = mm(hs, wk, "dual")
            v32, v16 = mm(hs, wv, "dual")
            fg = mm(hs, wf, "f32")
            qm_s = mm(hs, wqm, "scale_bf16", scale)
            g_s = mm(hs, wz, "silu_bf16")
            fgt = fg.reshape(Bs, L, LANE).transpose(0, 2, 1)
            clf_t = cache_fox_logf[j].astype(F32).transpose(0, 2, 1).reshape(Bs * H, P)
            rsum = suffix_sum(clf_t).reshape(Bs, H, P)
            mix_s, logf_s = fox_sample(q, k16, v16, cache_fox_k[j], cache_fox_v[j],
                                       fg, fgt, bf_pad, bf_pad.reshape(LANE, 1), rsum, Bs, L)
            s_k.append(k32.reshape(Bs, L, H, HEAD_DIM))
            s_v.append(v32.reshape(Bs, L, H, HEAD_DIM))
            s_lf.append(logf_s[:, :H].reshape(Bs, L, H))
        else:
            w = w_in_gdn[j]
            wqkv = w[:, :QKV].astype(BF16)
            wab = jnp.concatenate([_pad_cols(w[:, QKV:QKV + H], LANE),
                                   _pad_cols(w[:, QKV + H:QKV + 2 * H], LANE)], axis=1).astype(BF16)
            wqm = w[:, QKV + 2 * H:QKV + 2 * H + MEMW].astype(BF16)
            wz = w[:, QKV + 2 * H + MEMW:].astype(BF16)
            alog_pad = _pad_vec(gdn_A_log[j], LANE)
            dtb_pad = _pad_vec(gdn_dt_bias[j], LANE)

            qkv = mm(hp, wqkv, "f32")
            ab = mm(hp, wab, "f32")
            qm_p = mm(hp, wqm, "scale_bf16", scale)
            g_p = mm(hp, wz, "silu_bf16")
            gc, beta, gr = gdn_prep(ab, alog_pad, dtb_pad, B, T, rb_p, min(CHUNK, T))
            mix_p, S, conv = gdn_mixer(qkv, gdn_conv_w[j], gc, beta, gr,
                                       jnp.zeros((B, H, HEAD_DIM, HEAD_DIM), F32),
                                       jnp.zeros((B, 3, QKV), F32),
                                       gdn_out_norm[j], B, T, rb_p, min(CHUNK, T), 2)
            p_S.append(S)
            p_conv.append(conv)

            qkv = mm(hs, wqkv, "f32")
            ab = mm(hs, wab, "f32")
            qm_s = mm(hs, wqm, "scale_bf16", scale)
            g_s = mm(hs, wz, "silu_bf16")
            gc, beta, gr = gdn_prep(ab, alog_pad, dtb_pad, Bs, L, L, L)
            mix_s, S, conv = gdn_mixer(qkv, gdn_conv_w[j], gc, beta, gr,
                                       state_gdn_S[j].astype(F32), state_gdn_conv[j].astype(F32),
                                       gdn_out_norm[j], Bs, L, L, L, H)
            s_S.append(S)
            s_conv.append(conv)

        m_p = mem_attn(qm_p, mk_p, mv_p, T)
        m_s = mem_attn(qm_s, cache_mem_k[i], cache_mem_v[i], L)
        xp = out_proj(mix_p, m_p, g_p, w_o, xp)
        xs = out_proj(mix_s, m_s, g_s, w_o, xs)

    y_prompt = rms_norm_rows(xp, final_norm, F32).reshape(B, T, D)
    y_sample = rms_norm_rows(xs, final_norm, F32).reshape(Bs, L, D)
    mem_shape = (depth, B, NM, N_MEM_HEADS, HEAD_DIM)
    return (y_prompt, y_sample,
            jnp.stack(p_k), jnp.stack(p_v), jnp.stack(p_lf), jnp.stack(p_S), jnp.stack(p_conv),
            jnp.stack(p_mk).reshape(mem_shape), jnp.stack(p_mv).reshape(mem_shape),
            jnp.stack(s_k), jnp.stack(s_v), jnp.stack(s_lf), jnp.stack(s_S), jnp.stack(s_conv))
```

```python
import functools
import math

import jax
import jax.numpy as jnp
from jax import lax
from jax.experimental import pallas as pl
from jax.experimental.pallas import tpu as pltpu

F32 = jnp.float32
BF16 = jnp.bfloat16
HI = lax.Precision.HIGHEST

LANE = 128
HEAD_DIM = 128
N_HEADS = 12
MIX = N_HEADS * HEAD_DIM
N_MEM_HEADS = 4
MEMW = N_MEM_HEADS * HEAD_DIM
CHUNK = 64
EPS = 1e-6
LOG2E = math.log2(math.e)
VMEM_LIMIT = 48 * 1024 * 1024
VMEM_LIMIT_BIG = 58 * 1024 * 1024

NN_DIMS = (((1,), (0,)), ((), ()))
NT_DIMS = (((1,), (1,)), ((), ()))


def _params(*sem, vmem=VMEM_LIMIT):
    return pltpu.CompilerParams(dimension_semantics=sem, vmem_limit_bytes=vmem)


def _sigmoid(x):
    return 1.0 / (1.0 + jnp.exp(-x))


def _log_sigmoid(x):
    return jnp.minimum(x, 0.0) - jnp.log1p(jnp.exp(-jnp.abs(x)))


def _softplus(x):
    return jnp.maximum(x, 0.0) + jnp.log1p(jnp.exp(-jnp.abs(x)))


def _dot_hi(a, b):
    return jnp.dot(a, b, precision=HI, preferred_element_type=F32)


def _dot_nt_hi(a, b):
    return lax.dot_general(a, b, NT_DIMS, precision=HI, preferred_element_type=F32)


def _split_bf16(a):
    hi = a.astype(BF16)
    lo = (a - hi.astype(F32)).astype(BF16)
    return hi, lo


def _mm1(a, b, dims=NN_DIMS):
    return lax.dot_general(a.astype(BF16), b.astype(BF16), dims, preferred_element_type=F32)


def _mm2r(a, b):
    b_hi, b_lo = _split_bf16(b)
    aa = a.astype(BF16)
    return (jnp.dot(aa, b_hi, preferred_element_type=F32)
            + jnp.dot(aa, b_lo, preferred_element_type=F32))


def _iota2(shape, dim):
    return lax.broadcasted_iota(jnp.int32, shape, dim)


def _rms_kernel(x_ref, g_ref, o_ref):
    x = x_ref[...]
    y = x * lax.rsqrt(jnp.mean(x * x, axis=-1, keepdims=True) + EPS)
    o_ref[...] = (y * g_ref[...]).astype(o_ref.dtype)


def rms_norm_rows(x, gain, out_dtype):
    R, D = x.shape
    tm = min(R, 512)
    return pl.pallas_call(
        _rms_kernel,
        grid=(R // tm,),
        in_specs=[pl.BlockSpec((tm, D), lambda i: (i, 0)),
                  pl.BlockSpec((1, D), lambda i: (0, 0))],
        out_specs=pl.BlockSpec((tm, D), lambda i: (i, 0)),
        out_shape=jax.ShapeDtypeStruct((R, D), out_dtype),
        compiler_params=_params("parallel"),
        name="rms_norm",
    )(x, gain.reshape(1, D))


def _mm_kernel(h_ref, w_ref, *o_refs, kind, scale):
    acc = jnp.dot(h_ref[...], w_ref[...], preferred_element_type=F32)
    if kind == "f32":
        o_refs[0][...] = acc
    elif kind == "scale_bf16":
        o_refs[0][...] = (acc * scale).astype(BF16)
    elif kind == "silu_bf16":
        o_refs[0][...] = (acc * _sigmoid(acc)).astype(BF16)
    else:
        raise ValueError(kind)


def mm(h, w, kind, scale=1.0):
    R, K = h.shape
    N = w.shape[1]
    tm = min(R, 1024)
    tn = min(N, 512)
    out_dtypes = {"f32": [F32], "scale_bf16": [BF16], "silu_bf16": [BF16]}[kind]
    outs = pl.pallas_call(
        functools.partial(_mm_kernel, kind=kind, scale=scale),
        grid=(R // tm, N // tn),
        in_specs=[pl.BlockSpec((tm, K), lambda i, n: (i, 0)),
                  pl.BlockSpec((K, tn), lambda i, n: (0, n))],
        out_specs=[pl.BlockSpec((tm, tn), lambda i, n: (i, n)) for _ in out_dtypes],
        out_shape=[jax.ShapeDtypeStruct((R, N), dt) for dt in out_dtypes],
        compiler_params=_params("parallel", "arbitrary"),
        name="proj_" + kind,
    )(h, w)
    return outs if len(outs) > 1 else outs[0]


def _proj_kv_kernel(h_ref, w_ref, *refs):
    o32_ref, o16_ref = refs[-2], refs[-1]
    hv = h_ref[...]
    for hp in range(N_HEADS // 2):
        cs = slice(hp * 2 * HEAD_DIM, (hp + 1) * 2 * HEAD_DIM)
        acc = jnp.dot(hv, w_ref[:, cs], preferred_element_type=F32)
        o16_ref[:, cs] = acc.astype(BF16)
        o32_ref[:, 2 * hp, :] = acc[:, :HEAD_DIM]
        o32_ref[:, 2 * hp + 1, :] = acc[:, HEAD_DIM:]


def proj_kv(h, w, stacked, layer, n_layers):
    R, K = h.shape
    tm = min(R, 512)
    in_specs = [pl.BlockSpec((tm, K), lambda i: (i, 0)),
                pl.BlockSpec((K, MIX), lambda i: (0, 0))]
    args = [h, w]
    aliases = {}
    if stacked is not None:
        in_specs.append(pl.BlockSpec(memory_space=pl.ANY))
        args.append(stacked)
        aliases = {2: 0}
    return pl.pallas_call(
        _proj_kv_kernel,
        grid=(R // tm,),
        in_specs=in_specs,
        out_specs=[pl.BlockSpec((None, tm, N_HEADS, HEAD_DIM), lambda i: (layer, i, 0, 0)),
                   pl.BlockSpec((tm, MIX), lambda i: (i, 0))],
        out_shape=[jax.ShapeDtypeStruct((n_layers, R, N_HEADS, HEAD_DIM), F32),
                   jax.ShapeDtypeStruct((R, MIX), BF16)],
        input_output_aliases=aliases,
        compiler_params=_params("parallel"),
        name="proj_kv",
    )(*args)


def _out_proj_kernel(mix_ref, m_ref, g_ref, w_ref, x_ref, gain_ref, *o_refs):
    g = g_ref[...].astype(F32)
    br = jnp.concatenate([(mix_ref[...].astype(F32) * g[:, :MIX]).astype(BF16),
                          (m_ref[...].astype(F32) * g[:, MIX:]).astype(BF16)], axis=1)
    x = x_ref[...] + jnp.dot(br, w_ref[...], preferred_element_type=F32)
    y = x * lax.rsqrt(jnp.mean(x * x, axis=-1, keepdims=True) + EPS) * gain_ref[...]
    o_refs[-1][...] = y.astype(o_refs[-1].dtype)
    if len(o_refs) == 2:
        o_refs[0][...] = x


def out_proj(mix, m, g, w, x, gain, last):
    R, D = x.shape
    BW = MIX + MEMW
    tm = min(R, 256)
    row = lambda width: pl.BlockSpec((tm, width), lambda i: (i, 0))
    if last:
        out_specs, out_shape = [row(D)], [jax.ShapeDtypeStruct((R, D), F32)]
    else:
        out_specs = [row(D), row(D)]
        out_shape = [jax.ShapeDtypeStruct((R, D), F32), jax.ShapeDtypeStruct((R, D), BF16)]
    outs = pl.pallas_call(
        _out_proj_kernel,
        grid=(R // tm,),
        in_specs=[row(MIX), row(MEMW), row(BW),
                  pl.BlockSpec((BW, D), lambda i: (0, 0)),
                  row(D),
                  pl.BlockSpec((1, D), lambda i: (0, 0))],
        out_specs=out_specs,
        out_shape=out_shape,
        compiler_params=_params("parallel"),
        name="out_proj",
    )(mix, m, g, w, x, gain.reshape(1, D))
    return outs[0] if last else outs


def _mem_attn_kernel(q_ref, k_ref, v_ref, o_ref, *, per_head_kv):
    for h in range(N_MEM_HEADS):
        sl = slice(h * HEAD_DIM, (h + 1) * HEAD_DIM)
        q = q_ref[:, sl]
        if per_head_kv:
            k = k_ref[:, h, :].astype(BF16)
            v = v_ref[:, h, :].astype(BF16)
        else:
            k = k_ref[:, sl].astype(BF16)
            v = v_ref[:, sl].astype(BF16)
        s = lax.dot_general(q, k, NT_DIMS, preferred_element_type=F32)
        m = jnp.max(s, axis=1, keepdims=True)
        p = jnp.exp(s - m)
        l = jnp.sum(p, axis=1, keepdims=True)
        o = jnp.dot(p.astype(BF16), v, preferred_element_type=F32) / l
        o_ref[:, sl] = o.astype(BF16)


def mem_attn(q, mk, mv, rows_per_batch, layer=None):
    R = q.shape[0]
    per_head_kv = layer is not None
    tr = min(rows_per_batch, 1024)
    nrb = rows_per_batch // tr
    if per_head_kv:
        NB, NM = mk.shape[1:3]
        kv_spec = pl.BlockSpec((None, None, NM, N_MEM_HEADS, HEAD_DIM), lambda b, i: (layer, b, 0, 0, 0))
    else:
        NB, NM = mk.shape[:2]
        kv_spec = pl.BlockSpec((None, NM, MEMW), lambda b, i: (b, 0, 0))
    return pl.pallas_call(
        functools.partial(_mem_attn_kernel, per_head_kv=per_head_kv),
        grid=(NB, nrb),
        in_specs=[pl.BlockSpec((tr, MEMW), lambda b, i: (b * nrb + i, 0)), kv_spec, kv_spec],
        out_specs=pl.BlockSpec((tr, MEMW), lambda b, i: (b * nrb + i, 0)),
        out_shape=jax.ShapeDtypeStruct((R, MEMW), BF16),
        compiler_params=_params("parallel", "parallel"),
        name="mem_attn",
    )(q, mk, mv)


def _fox_prep_kernel(fg_ref, bf_ref, logf_ref, qa_ref, ka_ref, carry_ref, *, tp):
    @pl.when(pl.program_id(1) == 0)
    def _():
        carry_ref[...] = jnp.zeros_like(carry_ref)

    logf = _log_sigmoid(fg_ref[...] + bf_ref[...])
    tril = (_iota2((tp, tp), 1) <= _iota2((tp, tp), 0)).astype(F32)
    F = _dot_hi(tril, logf) + carry_ref[...]
    carry_ref[...] = F[tp - 1:tp, :]
    logf_ref[...] = logf

    F2 = F * LOG2E
    hi = F2.astype(BF16)
    r1 = F2 - hi.astype(F32)
    mid = r1.astype(BF16)
    lo = (r1 - mid.astype(F32)).astype(BF16)
    col = _iota2((LANE, MIX), 1)
    own_head = _iota2((LANE, MIX), 0) == (col >> 7)
    slot = col & (LANE - 1)

    def spread(x, c):
        sel = jnp.logical_and(own_head, slot == c).astype(BF16)
        return jnp.dot(x, sel, preferred_element_type=F32)

    slot_row = _iota2((1, MIX), 1) & (LANE - 1)
    ones_q = jnp.logical_and(slot_row >= 3, slot_row < 6).astype(F32)
    ones_k = (slot_row < 3).astype(F32)
    qa_ref[...] = (spread(hi, 0) + spread(mid, 1) + spread(lo, 2) + ones_q).astype(BF16)
    ka_ref[...] = (ones_k - (spread(hi, 3) + spread(mid, 4) + spread(lo, 5))).astype(BF16)


def fox_prep(fg, bf_pad, B, T, tp):
    nb = T // tp
    return pl.pallas_call(
        functools.partial(_fox_prep_kernel, tp=tp),
        grid=(B, nb),
        in_specs=[pl.BlockSpec((tp, LANE), lambda b, i: (b * nb + i, 0)),
                  pl.BlockSpec((1, LANE), lambda b, i: (0, 0))],
        out_specs=[pl.BlockSpec((tp, LANE), lambda b, i: (b * nb + i, 0)),
                   pl.BlockSpec((tp, MIX), lambda b, i: (b * nb + i, 0)),
                   pl.BlockSpec((tp, MIX), lambda b, i: (b * nb + i, 0))],
        out_shape=[jax.ShapeDtypeStruct((B * T, LANE), F32),
                   jax.ShapeDtypeStruct((B * T, MIX), BF16),
                   jax.ShapeDtypeStruct((B * T, MIX), BF16)],
        scratch_shapes=[pltpu.VMEM((1, LANE), F32)],
        compiler_params=_params("parallel", "arbitrary"),
        name="fox_prep",
    )(fg, bf_pad)


def _fox_attn_kernel(q_ref, qa_ref, k_ref, ka_ref, v_ref, o_ref, *, tq, tk, n_sub):
    qi = pl.program_id(2)
    sub = tq // n_sub
    qc = jnp.concatenate([q_ref[...], qa_ref[...]], axis=1)
    col = _iota2((sub, tk), 1)
    rows = [qi * tq + s * sub + _iota2((sub, tk), 0) for s in range(n_sub)]

    def body(kj, stats, masked):
        off = pl.multiple_of(kj * tk, tk)
        kc = jnp.concatenate([k_ref[pl.ds(off, tk), :], ka_ref[pl.ds(off, tk), :]], axis=1)
        vb = v_ref[pl.ds(off, tk), :]
        sc_all = lax.dot_general(qc, kc, NT_DIMS, preferred_element_type=F32)
        new = []
        for s in range(n_sub):
            m, l, acc = stats[s]
            sc = sc_all[s * sub:(s + 1) * sub, :]
            if masked:
                sc = jnp.where(col + kj * tk <= rows[s], sc, -jnp.inf)
            m_new = jnp.maximum(m, jnp.max(sc, axis=1, keepdims=True))
            alpha = jnp.exp2(m - m_new)
            p = jnp.exp2(sc - m_new)
            l = alpha * l + jnp.sum(p, axis=1, keepdims=True)
            acc = alpha * acc + jnp.dot(p.astype(BF16), vb, preferred_element_type=F32)
            new.append((m_new, l, acc))
        return tuple(new)

    n_full = (qi * tq + 1) // tk
    n_all = ((qi + 1) * tq + tk - 1) // tk
    stats = tuple((jnp.full((sub, 1), -jnp.inf, F32), jnp.zeros((sub, 1), F32), jnp.zeros((sub, HEAD_DIM), F32))
                  for _ in range(n_sub))
    stats = lax.fori_loop(0, n_full, functools.partial(body, masked=False), stats)
    stats = lax.fori_loop(n_full, n_all, functools.partial(body, masked=True), stats)
    for s in range(n_sub):
        m, l, acc = stats[s]
        o_ref[s * sub:(s + 1) * sub, :] = (acc / l).astype(BF16)


def fox_attn(q, qa, k, ka, v, B, T, tq, tk):
    nq = T // tq
    n_sub = 2 if tq >= 512 else 1
    q_spec = pl.BlockSpec((tq, HEAD_DIM), lambda b, h, i: (b * nq + i, h))
    kv_spec = pl.BlockSpec((T, HEAD_DIM), lambda b, h, i: (b, h))
    return pl.pallas_call(
        functools.partial(_fox_attn_kernel, tq=tq, tk=tk, n_sub=n_sub),
        grid=(B, N_HEADS, nq),
        in_specs=[q_spec, q_spec, kv_spec, kv_spec, kv_spec],
        out_specs=pl.BlockSpec((tq, HEAD_DIM), lambda b, h, i: (b * nq + i, h)),
        out_shape=jax.ShapeDtypeStruct((B * T, MIX), BF16),
        compiler_params=_params("parallel", "parallel", "parallel"),
        name="fox_attn",
    )(q, qa, k, ka, v)


def _suffix_sum_kernel(x_ref, o_ref):
    P = x_ref.shape[1]
    after = (_iota2((P, P), 0) > _iota2((P, P), 1)).astype(F32)
    o_ref[...] = _dot_hi(x_ref[...], after)


def suffix_sum(x):
    return pl.pallas_call(
        _suffix_sum_kernel,
        out_shape=jax.ShapeDtypeStruct(x.shape, F32),
        compiler_params=pltpu.CompilerParams(vmem_limit_bytes=VMEM_LIMIT),
        name="fox_suffix_sum",
    )(x)


def _fox_sample_kernel(q_ref, kn_ref, vn_ref, ck_ref, cv_ref, fg_ref, fgt_ref, bf_ref, bft_ref, r_ref,
                       o_ref, logf_ref, *, L):
    logf = _log_sigmoid(fg_ref[...] + bf_ref[...])
    logf_ref[...] = logf
    r_i = _iota2((L, L), 0)
    c_i = _iota2((L, L), 1)
    causal = c_i <= r_i
    ln_col = _dot_hi(causal.astype(F32), logf)
    logf_t = _log_sigmoid(fgt_ref[0] + bft_ref[...])
    ln_row = _dot_hi(logf_t, (r_i <= c_i).astype(F32))
    for h in range(N_HEADS):
        sl = slice(h * HEAD_DIM, (h + 1) * HEAD_DIM)
        q = q_ref[:, sl]
        kc = ck_ref[:, h, :].astype(BF16)
        vc = cv_ref[:, h, :].astype(BF16)
        fq = ln_col[:, h:h + 1]
        sc = lax.dot_general(q, kc, NT_DIMS, preferred_element_type=F32) + (fq + r_ref[0, h:h + 1, :])
        sn = lax.dot_general(q, kn_ref[:, sl], NT_DIMS, preferred_element_type=F32) + (fq - ln_row[h:h + 1, :])
        sn = jnp.where(causal, sn, -jnp.inf)
        m = jnp.maximum(jnp.max(sc, axis=1, keepdims=True), jnp.max(sn, axis=1, keepdims=True))
        pc = jnp.exp(sc - m)
        pn = jnp.exp(sn - m)
        l = jnp.sum(pc, axis=1, keepdims=True) + jnp.sum(pn, axis=1, keepdims=True)
        o = (jnp.dot(pc.astype(BF16), vc, preferred_element_type=F32)
             + jnp.dot(pn.astype(BF16), vn_ref[:, sl], preferred_element_type=F32)) / l
        o_ref[:, sl] = o.astype(BF16)


def fox_sample(q, kn, vn, ck, cv, layer, fg, fgt, bf_pad, bft, rsum, Bs, L):
    P = ck.shape[2]
    cache_spec = pl.BlockSpec((None, None, P, N_HEADS, HEAD_DIM), lambda b: (layer, b, 0, 0, 0))
    return pl.pallas_call(
        functools.partial(_fox_sample_kernel, L=L),
        grid=(Bs,),
        in_specs=[pl.BlockSpec((L, MIX), lambda b: (b, 0)),
                  pl.BlockSpec((L, MIX), lambda b: (b, 0)),
                  pl.BlockSpec((L, MIX), lambda b: (b, 0)),
                  cache_spec, cache_spec,
                  pl.BlockSpec((L, LANE), lambda b: (b, 0)),
                  pl.BlockSpec((1, LANE, L), lambda b: (b, 0, 0)),
                  pl.BlockSpec((1, LANE), lambda b: (0, 0)),
                  pl.BlockSpec((LANE, 1), lambda b: (0, 0)),
                  pl.BlockSpec((1, N_HEADS, P), lambda b: (b, 0, 0))],
        out_specs=[pl.BlockSpec((L, MIX), lambda b: (b, 0)),
                   pl.BlockSpec((L, LANE), lambda b: (b, 0))],
        out_shape=[jax.ShapeDtypeStruct((Bs * L, MIX), BF16),
                   jax.ShapeDtypeStruct((Bs * L, LANE), F32)],
        compiler_params=_params("parallel", vmem=VMEM_LIMIT_BIG),
        name="fox_sample",
    )(q, kn, vn, ck, cv, fg, fgt, bf_pad, bft, rsum)


def _gdn_prep_kernel(ab_ref, alog_ref, dtb_ref, gc_ref, beta_ref, gr_ref, *, rb, C):
    a = ab_ref[:, :LANE]
    bt = ab_ref[:, LANE:]
    g = -jnp.exp(alog_ref[...]) * _softplus(a + dtb_ref[...])
    beta_ref[...] = _sigmoid(bt)
    r_i = _iota2((rb, rb), 0)
    c_i = _iota2((rb, rb), 1)
    same_chunk_tril = jnp.logical_and(r_i // C == c_i // C, c_i <= r_i).astype(F32)
    G = _dot_hi(same_chunk_tril, g)
    gc_ref[...] = G
    eye = (_iota2((LANE, LANE), 0) == _iota2((LANE, LANE), 1)).astype(F32)
    gr_ref[0, 0] = _dot_nt_hi(eye, G)


def gdn_prep(ab, alog_pad, dtb_pad, B, T, rb, C):
    nb = T // rb
    return pl.pallas_call(
        functools.partial(_gdn_prep_kernel, rb=rb, C=C),
        grid=(B, nb),
        in_specs=[pl.BlockSpec((rb, 2 * LANE), lambda b, i: (b * nb + i, 0)),
                  pl.BlockSpec((1, LANE), lambda b, i: (0, 0)),
                  pl.BlockSpec((1, LANE), lambda b, i: (0, 0))],
        out_specs=[pl.BlockSpec((rb, LANE), lambda b, i: (b * nb + i, 0)),
                   pl.BlockSpec((rb, LANE), lambda b, i: (b * nb + i, 0)),
                   pl.BlockSpec((1, 1, LANE, rb), lambda b, i: (b, i, 0, 0))],
        out_shape=[jax.ShapeDtypeStruct((B * T, LANE), F32),
                   jax.ShapeDtypeStruct((B * T, LANE), F32),
                   jax.ShapeDtypeStruct((B, nb, LANE, rb), F32)],
        compiler_params=_params("parallel", "parallel"),
        name="gdn_prep",
    )(ab, alog_pad, dtb_pad)


def _gdn_kernel(xq_ref, xk_ref, xv_ref, wq_ref, wk_ref, wv_ref, gc_ref, beta_ref, gr_ref, s0_ref,
                bq_ref, bk_ref, bv_ref, gain_ref,
                o_ref, sout_ref, cq_ref, ck_ref, cv_ref, S_ref, tail_ref, *, rb, C, nblk, hb):
    hblk = pl.program_id(1)
    i = pl.program_id(2)
    nc = rb // C
    W = hb * HEAD_DIM
    n_iter = C.bit_length() - 2

    @pl.when(i == 0)
    def _():
        S_ref[...] = s0_ref[...]
        tail_ref[...] = jnp.zeros_like(tail_ref)
        tail_ref[0, 5:8, :] = bq_ref[...]
        tail_ref[1, 5:8, :] = bk_ref[...]
        tail_ref[2, 5:8, :] = bv_ref[...]

    row8 = _iota2((8, W), 0)

    def conv_silu(x_ref, w_ref, idx):
        x = x_ref[...]
        w = w_ref[...]
        t8 = tail_ref[idx]
        y = x * w[3:4, :]
        for j in range(1, 4):
            xr = pltpu.roll(x, j, 0)
            head = jnp.where(row8 < j, pltpu.roll(t8, j, 0), xr[:8])
            xs = jnp.concatenate([head, xr[8:]], axis=0)
            y = y + xs * w[3 - j:4 - j, :]
        tail_ref[idx] = x[rb - 8:, :]
        return y * _sigmoid(y)

    qc_all = conv_silu(xq_ref, wq_ref, 0)
    kc_all = conv_silu(xk_ref, wk_ref, 1)
    vv_all = conv_silu(xv_ref, wv_ref, 2)

    r_i = _iota2((C, C), 0)
    c_i = _iota2((C, C), 1)
    incl = c_i <= r_i
    strict = c_i < r_i
    lane = _iota2((rb, LANE), 1)
    gc_blk = gc_ref[...]
    beta_blk = beta_ref[...]

    heads = []
    for hh in range(hb):
        hsl = slice(hh * HEAD_DIM, (hh + 1) * HEAD_DIM)
        qc = qc_all[:, hsl]
        kc = kc_all[:, hsl]
        qn = qc * lax.rsqrt(jnp.sum(qc * qc, axis=-1, keepdims=True) + EPS) * (HEAD_DIM ** -0.5)
        kn = kc * lax.rsqrt(jnp.sum(kc * kc, axis=-1, keepdims=True) + EPS)
        head_id = hblk * hb + hh
        sel = lane == head_id
        g_col = jnp.sum(jnp.where(sel, gc_blk, 0.0), axis=1, keepdims=True)
        b_col = jnp.sum(jnp.where(sel, beta_blk, 0.0), axis=1, keepdims=True)
        g_row = gr_ref[0, 0, pl.ds(head_id, 1), :]
        heads.append(dict(qn=qn, kn=kn, vv=vv_all[:, hsl], g_col=g_col, b_col=b_col, g_row=g_row,
                          eg_col=jnp.exp(g_col), S=S_ref[hh], outs=[]))

    units = []
    for c in range(nc):
        sl = slice(c * C, (c + 1) * C)
        for hd in heads:
            gi = hd["g_col"][sl]
            u_ = dict(hd=hd, gi=gi, bi=hd["b_col"][sl], egi=hd["eg_col"][sl],
                      k=hd["kn"][sl], q=hd["qn"][sl], v=hd["vv"][sl])
            u_["decay"] = jnp.exp(jnp.where(incl, gi - hd["g_row"][:, sl], -jnp.inf))
            units.append(u_)
    for u_ in units:
        u_["kq"] = _mm1(jnp.concatenate([u_["k"], u_["q"]], axis=0), u_["k"], NT_DIMS)
    for u_ in units:
        u_["P"] = -jnp.where(strict, u_["bi"] * u_["kq"][:C] * u_["decay"], 0.0)
        u_["qk"] = jnp.where(incl, u_["kq"][C:] * u_["decay"], 0.0)
        u_["X"] = jnp.concatenate([u_["v"] * u_["bi"], u_["k"] * (u_["bi"] * u_["egi"])], axis=1)
    for _ in range(n_iter):
        for u_ in units:
            u_["Y"] = _mm2r(u_["P"], jnp.concatenate([u_["X"], u_["P"]], axis=1))
        for u_ in units:
            u_["X"] = u_["X"] + u_["Y"][:, :2 * HEAD_DIM]
            u_["P"] = u_["Y"][:, 2 * HEAD_DIM:]
    for u_ in units:
        u_["Y"] = _mm2r(u_["P"], u_["X"])
    for u_ in units:
        X = u_["X"] + u_["Y"]
        g_last = u_["gi"][C - 1:C, :]
        u_["u"] = X[:, :HEAD_DIM]
        u_["wq"] = jnp.concatenate([X[:, HEAD_DIM:], u_["q"] * u_["egi"]], axis=0)
        u_["qk_kt"] = jnp.concatenate([u_["qk"], (u_["k"] * jnp.exp(g_last - u_["gi"])).T], axis=0)
        u_["g_last"] = jnp.exp(g_last)

    for c in range(nc):
        cu = units[c * hb:(c + 1) * hb]
        for u_ in cu:
            u_["ws_qs"] = _mm1(u_["wq"], u_["hd"]["S"])
        for u_ in cu:
            u_["v_new"] = u_["u"] - u_["ws_qs"][:C]
            u_["od"] = _mm2r(u_["qk_kt"], u_["v_new"])
        for u_ in cu:
            hd = u_["hd"]
            hd["outs"].append(u_["ws_qs"][C:] + u_["od"][:C])
            hd["S"] = hd["S"] * u_["g_last"] + u_["od"][C:]

    gain = gain_ref[...]
    for hh, hd in enumerate(heads):
        hsl = slice(hh * HEAD_DIM, (hh + 1) * HEAD_DIM)
        S_ref[hh] = hd["S"]
        o = hd["outs"][0] if nc == 1 else jnp.concatenate(hd["outs"], axis=0)
        y = o * lax.rsqrt(jnp.mean(o * o, axis=-1, keepdims=True) + EPS) * gain
        o_ref[:, hsl] = y.astype(BF16)

    @pl.when(i == nblk - 1)
    def _():
        for hh, hd in enumerate(heads):
            sout_ref[0, hh] = hd["S"]
        cq_ref[0] = xq_ref[pl.ds(rb - 3, 3), :]
        ck_ref[0] = xk_ref[pl.ds(rb - 3, 3), :]
        cv_ref[0] = xv_ref[pl.ds(rb - 3, 3), :]


def gdn_mixer(qkv, conv_w, gc, beta, gr, S0, buf0, layer, out_gain, B, T, rb, C, hb):
    nblk = T // rb
    H = N_HEADS
    W = hb * HEAD_DIM
    nhb = H // hb
    x_spec = lambda off: pl.BlockSpec((rb, W), lambda b, h, i: (b * nblk + i, off + h))
    w_spec = lambda off: pl.BlockSpec((4, W), lambda b, h, i: (0, off + h))
    b_spec = lambda off: pl.BlockSpec((None, None, 3, W), lambda b, h, i: (layer, b, 0, off + h))
    col_spec = pl.BlockSpec((rb, LANE), lambda b, h, i: (b * nblk + i, 0))
    s_spec = pl.BlockSpec((1, hb, HEAD_DIM, HEAD_DIM), lambda b, h, i: (b, h, 0, 0))
    c_out = pl.BlockSpec((1, 3, W), lambda b, h, i: (b, 0, h))
    mix, S, cq, ck, cv = pl.pallas_call(
        functools.partial(_gdn_kernel, rb=rb, C=C, nblk=nblk, hb=hb),
        grid=(B, nhb, nblk),
        in_specs=[x_spec(0), x_spec(nhb), x_spec(2 * nhb), w_spec(0), w_spec(nhb), w_spec(2 * nhb),
                  col_spec, col_spec,
                  pl.BlockSpec((1, 1, LANE, rb), lambda b, h, i: (b, i, 0, 0)),
                  pl.BlockSpec((None, None, hb, HEAD_DIM, HEAD_DIM), lambda b, h, i: (layer, b, h, 0, 0)),
                  b_spec(0), b_spec(nhb), b_spec(2 * nhb),
                  pl.BlockSpec((1, HEAD_DIM), lambda b, h, i: (0, 0))],
        out_specs=[pl.BlockSpec((rb, W), lambda b, h, i: (b * nblk + i, h)),
                   s_spec, c_out, c_out, c_out],
        out_shape=[jax.ShapeDtypeStruct((B * T, MIX), BF16),
                   jax.ShapeDtypeStruct((B, H, HEAD_DIM, HEAD_DIM), F32),
                   jax.ShapeDtypeStruct((B, 3, MIX), F32),
                   jax.ShapeDtypeStruct((B, 3, MIX), F32),
                   jax.ShapeDtypeStruct((B, 3, MIX), F32)],
        scratch_shapes=[pltpu.VMEM((hb, HEAD_DIM, HEAD_DIM), F32),
                        pltpu.VMEM((3, 8, W), F32)],
        compiler_params=_params("parallel", "parallel", "arbitrary"),
        name="gdn_mixer",
    )(qkv, qkv, qkv, conv_w, conv_w, conv_w, gc, beta, gr, S0, buf0, buf0, buf0, out_gain.reshape(1, HEAD_DIM))
    return mix, S, jnp.concatenate([cq, ck, cv], axis=-1)


def _pad_cols(w, n):
    return jnp.pad(w, ((0, 0), (0, n - w.shape[1])))


def _pad_vec(v, n):
    return jnp.pad(v.astype(F32), (0, n - v.shape[0])).reshape(1, n)


def kernel(x_prompt, x_sample, mem_prompt, cache_fox_k, cache_fox_v, cache_fox_logf, state_gdn_S, state_gdn_conv, cache_mem_k, cache_mem_v, norm_gain, mem_norm_gain, w_mem_kv, w_in_fox, b_forget, w_in_gdn, gdn_conv_w, gdn_A_log, gdn_dt_bias, gdn_out_norm, w_out, final_norm):
    B, T, D = x_prompt.shape
    Bs, L, _ = x_sample.shape
    depth = norm_gain.shape[0]
    NM = mem_prompt.shape[1]
    P = cache_fox_k.shape[2]
    H = N_HEADS
    scale = HEAD_DIM ** -0.5
    QKV = 3 * MIX

    xp = x_prompt.reshape(B * T, D)
    xs = x_sample.reshape(Bs * L, D)
    mem = mem_prompt.reshape(B * NM, D)

    tq = min(T, 512)
    tk = min(T, 1024)
    n_fox = (depth + 1) // 2
    pk_all = pv_all = sk_all = sv_all = None
    rb_p = min(T, 4 * CHUNK)

    p_lf, p_S, p_conv, p_mk, p_mv = [], [], [], [], []
    s_lf, s_S, s_conv = [], [], []

    for i in range(depth):
        j = i // 2
        hm = rms_norm_rows(mem, mem_norm_gain[i], BF16)
        wkv = w_mem_kv[i].astype(BF16)
        mk_p = mm(hm, wkv[:, :MEMW], "f32").reshape(B, NM, MEMW)
        mv_p = mm(hm, wkv[:, MEMW:], "f32").reshape(B, NM, MEMW)
        p_mk.append(mk_p)
        p_mv.append(mv_p)

        if i == 0:
            hp = rms_norm_rows(xp, norm_gain[0], BF16)
            hs = rms_norm_rows(xs, norm_gain[0], BF16)
        w_o = w_out[i].astype(BF16)

        if i % 2 == 0:
            w = w_in_fox[j]
            wq = w[:, :MIX].astype(BF16)
            wk = w[:, MIX:2 * MIX].astype(BF16)
            wv = w[:, 2 * MIX:QKV].astype(BF16)
            wf = _pad_cols(w[:, QKV:QKV + H], LANE).astype(BF16)
            wqm = w[:, QKV + H:QKV + H + MEMW].astype(BF16)
            wz = w[:, QKV + H + MEMW:].astype(BF16)
            bf_pad = _pad_vec(b_forget[j], LANE)

            q = mm(hp, wq, "scale_bf16", scale * LOG2E)
            pk_all, k16 = proj_kv(hp, wk, pk_all, j, n_fox)
            pv_all, v16 = proj_kv(hp, wv, pv_all, j, n_fox)
            fg = mm(hp, wf, "f32")
            qm_p = mm(hp, wqm, "scale_bf16", scale)
            g_p = mm(hp, wz, "silu_bf16")
            logf, qa, ka = fox_prep(fg, bf_pad, B, T, min(T, 512))
            mix_p = fox_attn(q, qa, k16, ka, v16, B, T, tq, tk)
            p_lf.append(logf[:, :H].reshape(B, T, H))

            q = mm(hs, wq, "scale_bf16", scale)
            sk_all, k16 = proj_kv(hs, wk, sk_all, j, n_fox)
            sv_all, v16 = proj_kv(hs, wv, sv_all, j, n_fox)
            fg = mm(hs, wf, "f32")
            qm_s = mm(hs, wqm, "scale_bf16", scale)
            g_s = mm(hs, wz, "silu_bf16")
            fgt = fg.reshape(Bs, L, LANE).transpose(0, 2, 1)
            clf_t = cache_fox_logf[j].astype(F32).transpose(0, 2, 1).reshape(Bs * H, P)
            rsum = suffix_sum(clf_t).reshape(Bs, H, P)
            mix_s, logf_s = fox_sample(q, k16, v16, cache_fox_k, cache_fox_v, j,
                                       fg, fgt, bf_pad, bf_pad.reshape(LANE, 1), rsum, Bs, L)
            s_lf.append(logf_s[:, :H].reshape(Bs, L, H))
        else:
            w = w_in_gdn[j]
            wqkv = w[:, :QKV].astype(BF16)
            wab = jnp.concatenate([_pad_cols(w[:, QKV:QKV + H], LANE),
                                   _pad_cols(w[:, QKV + H:QKV + 2 * H], LANE)], axis=1).astype(BF16)
            wqm = w[:, QKV + 2 * H:QKV + 2 * H + MEMW].astype(BF16)
            wz = w[:, QKV + 2 * H + MEMW:].astype(BF16)
            alog_pad = _pad_vec(gdn_A_log[j], LANE)
            dtb_pad = _pad_vec(gdn_dt_bias[j], LANE)

            qkv = mm(hp, wqkv, "f32")
            ab = mm(hp, wab, "f32")
            qm_p = mm(hp, wqm, "scale_bf16", scale)
            g_p = mm(hp, wz, "silu_bf16")
            gc, beta, gr = gdn_prep(ab, alog_pad, dtb_pad, B, T, rb_p, min(CHUNK, T))
            mix_p, S, conv = gdn_mixer(qkv, gdn_conv_w[j], gc, beta, gr,
                                       jnp.zeros((1, B, H, HEAD_DIM, HEAD_DIM), F32),
                                       jnp.zeros((1, B, 3, QKV), F32), 0,
                                       gdn_out_norm[j], B, T, rb_p, min(CHUNK, T), 2)
            p_S.append(S)
            p_conv.append(conv)

            qkv = mm(hs, wqkv, "f32")
            ab = mm(hs, wab, "f32")
            qm_s = mm(hs, wqm, "scale_bf16", scale)
            g_s = mm(hs, wz, "silu_bf16")
            gc, beta, gr = gdn_prep(ab, alog_pad, dtb_pad, Bs, L, L, L)
            mix_s, S, conv = gdn_mixer(qkv, gdn_conv_w[j], gc, beta, gr, state_gdn_S, state_gdn_conv, j,
                                       gdn_out_norm[j], Bs, L, L, L, H)
            s_S.append(S)
            s_conv.append(conv)

        m_p = mem_attn(qm_p, mk_p, mv_p, T)
        m_s = mem_attn(qm_s, cache_mem_k, cache_mem_v, L, layer=i)
        if i + 1 < depth:
            xp, hp = out_proj(mix_p, m_p, g_p, w_o, xp, norm_gain[i + 1], False)
            xs, hs = out_proj(mix_s, m_s, g_s, w_o, xs, norm_gain[i + 1], False)
        else:
            y_prompt = out_proj(mix_p, m_p, g_p, w_o, xp, final_norm, True).reshape(B, T, D)
            y_sample = out_proj(mix_s, m_s, g_s, w_o, xs, final_norm, True).reshape(Bs, L, D)

    mem_shape = (depth, B, NM, N_MEM_HEADS, HEAD_DIM)
    return (y_prompt, y_sample,
            pk_all.reshape(n_fox, B, T, H, HEAD_DIM), pv_all.reshape(n_fox, B, T, H, HEAD_DIM), jnp.stack(p_lf), jnp.stack(p_S), jnp.stack(p_conv),
            jnp.stack(p_mk).reshape(mem_shape), jnp.stack(p_mv).reshape(mem_shape),
            sk_all.reshape(n_fox, Bs, L, H, HEAD_DIM), sv_all.reshape(n_fox, Bs, L, H, HEAD_DIM), jnp.stack(s_lf), jnp.stack(s_S), jnp.stack(s_conv))
```

```python
import functools
import math

import jax
import jax.numpy as jnp
from jax import lax
from jax.experimental import pallas as pl
from jax.experimental.pallas import tpu as pltpu

F32 = jnp.float32
BF16 = jnp.bfloat16
HI = lax.Precision.HIGHEST

LANE = 128
HEAD_DIM = 128
N_HEADS = 12
MIX = N_HEADS * HEAD_DIM
N_MEM_HEADS = 4
MEMW = N_MEM_HEADS * HEAD_DIM
CHUNK = 64
EPS = 1e-6
LOG2E = math.log2(math.e)
VMEM_LIMIT = 48 * 1024 * 1024

NN_DIMS = (((1,), (0,)), ((), ()))
NT_DIMS = (((1,), (1,)), ((), ()))


def _params(*sem):
    return pltpu.CompilerParams(dimension_semantics=sem, vmem_limit_bytes=VMEM_LIMIT)


def _sigmoid(x):
    return 1.0 / (1.0 + jnp.exp(-x))


def _log_sigmoid(x):
    return jnp.minimum(x, 0.0) - jnp.log1p(jnp.exp(-jnp.abs(x)))


def _softplus(x):
    return jnp.maximum(x, 0.0) + jnp.log1p(jnp.exp(-jnp.abs(x)))


def _dot_hi(a, b):
    return jnp.dot(a, b, precision=HI, preferred_element_type=F32)


def _dot_nt_hi(a, b):
    return lax.dot_general(a, b, NT_DIMS, precision=HI, preferred_element_type=F32)


def _split_bf16(a):
    hi = a.astype(BF16)
    lo = (a - hi.astype(F32)).astype(BF16)
    return hi, lo


def _mm1(a, b, dims=NN_DIMS):
    return lax.dot_general(a.astype(BF16), b.astype(BF16), dims, preferred_element_type=F32)


def _mm2r(a, b):
    b_hi, b_lo = _split_bf16(b)
    aa = a.astype(BF16)
    return (jnp.dot(aa, b_hi, preferred_element_type=F32)
            + jnp.dot(aa, b_lo, preferred_element_type=F32))


def _iota2(shape, dim):
    return lax.broadcasted_iota(jnp.int32, shape, dim)


def _rms_kernel(x_ref, g_ref, o_ref):
    x = x_ref[...]
    y = x * lax.rsqrt(jnp.mean(x * x, axis=-1, keepdims=True) + EPS)
    o_ref[...] = (y * g_ref[...]).astype(o_ref.dtype)


def rms_norm_rows(x, gain, out_dtype):
    R, D = x.shape
    tm = min(R, 512)
    return pl.pallas_call(
        _rms_kernel,
        grid=(R // tm,),
        in_specs=[pl.BlockSpec((tm, D), lambda i: (i, 0)),
                  pl.BlockSpec((1, D), lambda i: (0, 0))],
        out_specs=pl.BlockSpec((tm, D), lambda i: (i, 0)),
        out_shape=jax.ShapeDtypeStruct((R, D), out_dtype),
        compiler_params=_params("parallel"),
        name="rms_norm",
    )(x, gain.reshape(1, D))


def _mm_kernel(h_ref, w_ref, *o_refs, kind, scale):
    acc = jnp.dot(h_ref[...], w_ref[...], preferred_element_type=F32)
    if kind == "f32":
        o_refs[0][...] = acc
    elif kind == "scale_bf16":
        o_refs[0][...] = (acc * scale).astype(BF16)
    elif kind == "silu_bf16":
        o_refs[0][...] = (acc * _sigmoid(acc)).astype(BF16)
    else:
        raise ValueError(kind)


def mm(h, w, kind, scale=1.0):
    R, K = h.shape
    N = w.shape[1]
    tm = min(R, 1024)
    tn = min(N, 512)
    out_dtypes = {"f32": [F32], "scale_bf16": [BF16], "silu_bf16": [BF16]}[kind]
    outs = pl.pallas_call(
        functools.partial(_mm_kernel, kind=kind, scale=scale),
        grid=(R // tm, N // tn),
        in_specs=[pl.BlockSpec((tm, K), lambda i, n: (i, 0)),
                  pl.BlockSpec((K, tn), lambda i, n: (0, n))],
        out_specs=[pl.BlockSpec((tm, tn), lambda i, n: (i, n)) for _ in out_dtypes],
        out_shape=[jax.ShapeDtypeStruct((R, N), dt) for dt in out_dtypes],
        compiler_params=_params("parallel", "arbitrary"),
        name="proj_" + kind,
    )(h, w)
    return outs if len(outs) > 1 else outs[0]


def _proj_kv_kernel(h_ref, w_ref, *refs):
    o32_ref, o16_ref = refs[-2], refs[-1]
    nb, _, tt, _ = o32_ref.shape
    acc = jnp.dot(h_ref[...], w_ref[...], preferred_element_type=F32)
    o16_ref[...] = acc.astype(BF16)
    o32_ref[:, 0] = acc[:, :HEAD_DIM].reshape(nb, tt, HEAD_DIM)
    o32_ref[:, 1] = acc[:, HEAD_DIM:].reshape(nb, tt, HEAD_DIM)


def proj_kv(h, w, stacked, layer, n_layers, NB, T):
    R, K = h.shape
    tm = min(R, 1024)
    tt = min(tm, T)
    nb = tm // tt
    nt = T // tt
    in_specs = [pl.BlockSpec((tm, K), lambda i, hp: (i, 0)),
                pl.BlockSpec((K, 2 * HEAD_DIM), lambda i, hp: (0, hp))]
    args = [h, w]
    aliases = {}
    if stacked is not None:
        in_specs.append(pl.BlockSpec(memory_space=pl.ANY))
        args.append(stacked)
        aliases = {2: 0}
    return pl.pallas_call(
        _proj_kv_kernel,
        grid=(R // tm, N_HEADS // 2),
        in_specs=in_specs,
        out_specs=[pl.BlockSpec((None, nb, 2, tt, HEAD_DIM), lambda i, hp: (layer, i // nt, hp, i % nt, 0)),
                   pl.BlockSpec((tm, 2 * HEAD_DIM), lambda i, hp: (i, hp))],
        out_shape=[jax.ShapeDtypeStruct((n_layers, NB, N_HEADS, T, HEAD_DIM), F32),
                   jax.ShapeDtypeStruct((R, MIX), BF16)],
        input_output_aliases=aliases,
        compiler_params=_params("parallel", "arbitrary"),
        name="proj_kv",
    )(*args)


def _out_proj_kernel(mix_ref, m_ref, g_ref, w_ref, x_ref, gain_ref, *o_refs):
    g = g_ref[...].astype(F32)
    br = jnp.concatenate([(mix_ref[...].astype(F32) * g[:, :MIX]).astype(BF16),
                          (m_ref[...].astype(F32) * g[:, MIX:]).astype(BF16)], axis=1)
    x = x_ref[...] + jnp.dot(br, w_ref[...], preferred_element_type=F32)
    y = x * lax.rsqrt(jnp.mean(x * x, axis=-1, keepdims=True) + EPS) * gain_ref[...]
    o_refs[-1][...] = y.astype(o_refs[-1].dtype)
    if len(o_refs) == 2:
        o_refs[0][...] = x


def out_proj(mix, m, g, w, x, gain, last):
    R, D = x.shape
    BW = MIX + MEMW
    tm = min(R, 256)
    row = lambda width: pl.BlockSpec((tm, width), lambda i: (i, 0))
    if last:
        out_specs, out_shape = [row(D)], [jax.ShapeDtypeStruct((R, D), F32)]
    else:
        out_specs = [row(D), row(D)]
        out_shape = [jax.ShapeDtypeStruct((R, D), F32), jax.ShapeDtypeStruct((R, D), BF16)]
    outs = pl.pallas_call(
        _out_proj_kernel,
        grid=(R // tm,),
        in_specs=[row(MIX), row(MEMW), row(BW),
                  pl.BlockSpec((BW, D), lambda i: (0, 0)),
                  row(D),
                  pl.BlockSpec((1, D), lambda i: (0, 0))],
        out_specs=out_specs,
        out_shape=out_shape,
        compiler_params=_params("parallel"),
        name="out_proj",
    )(mix, m, g, w, x, gain.reshape(1, D))
    return outs[0] if last else outs


def _mem_attn_kernel(q_ref, k_ref, v_ref, o_ref, *, per_head_kv):
    for h in range(N_MEM_HEADS):
        sl = slice(h * HEAD_DIM, (h + 1) * HEAD_DIM)
        q = q_ref[:, sl]
        if per_head_kv:
            k = k_ref[:, h, :].astype(BF16)
            v = v_ref[:, h, :].astype(BF16)
        else:
            k = k_ref[:, sl].astype(BF16)
            v = v_ref[:, sl].astype(BF16)
        s = lax.dot_general(q, k, NT_DIMS, preferred_element_type=F32)
        m = jnp.max(s, axis=1, keepdims=True)
        p = jnp.exp(s - m)
        l = jnp.sum(p, axis=1, keepdims=True)
        o = jnp.dot(p.astype(BF16), v, preferred_element_type=F32) / l
        o_ref[:, sl] = o.astype(BF16)


def mem_attn(q, mk, mv, rows_per_batch, layer=None):
    R = q.shape[0]
    per_head_kv = layer is not None
    tr = min(rows_per_batch, 1024)
    nrb = rows_per_batch // tr
    if per_head_kv:
        NB, NM = mk.shape[1:3]
        kv_spec = pl.BlockSpec((None, None, NM, N_MEM_HEADS, HEAD_DIM), lambda b, i: (layer, b, 0, 0, 0))
    else:
        NB, NM = mk.shape[:2]
        kv_spec = pl.BlockSpec((None, NM, MEMW), lambda b, i: (b, 0, 0))
    return pl.pallas_call(
        functools.partial(_mem_attn_kernel, per_head_kv=per_head_kv),
        grid=(NB, nrb),
        in_specs=[pl.BlockSpec((tr, MEMW), lambda b, i: (b * nrb + i, 0)), kv_spec, kv_spec],
        out_specs=pl.BlockSpec((tr, MEMW), lambda b, i: (b * nrb + i, 0)),
        out_shape=jax.ShapeDtypeStruct((R, MEMW), BF16),
        compiler_params=_params("parallel", "parallel"),
        name="mem_attn",
    )(q, mk, mv)


def _fox_prep_kernel(fg_ref, bf_ref, logf_ref, qa_ref, ka_ref, carry_ref, *, tp):
    @pl.when(pl.program_id(1) == 0)
    def _():
        carry_ref[...] = jnp.zeros_like(carry_ref)

    logf = _log_sigmoid(fg_ref[...] + bf_ref[...])
    tril = (_iota2((tp, tp), 1) <= _iota2((tp, tp), 0)).astype(F32)
    F = _dot_hi(tril, logf) + carry_ref[...]
    carry_ref[...] = F[tp - 1:tp, :]
    logf_ref[...] = logf

    F2 = F * LOG2E
    hi = F2.astype(BF16)
    r1 = F2 - hi.astype(F32)
    mid = r1.astype(BF16)
    lo = (r1 - mid.astype(F32)).astype(BF16)
    col = _iota2((LANE, MIX), 1)
    own_head = _iota2((LANE, MIX), 0) == (col >> 7)
    slot = col & (LANE - 1)

    def spread(x, c):
        sel = jnp.logical_and(own_head, slot == c).astype(BF16)
        return jnp.dot(x, sel, preferred_element_type=F32)

    slot_row = _iota2((1, MIX), 1) & (LANE - 1)
    ones_q = jnp.logical_and(slot_row >= 3, slot_row < 6).astype(F32)
    ones_k = (slot_row < 3).astype(F32)
    qa_ref[...] = (spread(hi, 0) + spread(mid, 1) + spread(lo, 2) + ones_q).astype(BF16)
    ka_ref[...] = (ones_k - (spread(hi, 3) + spread(mid, 4) + spread(lo, 5))).astype(BF16)


def fox_prep(fg, bf_pad, B, T, tp):
    nb = T // tp
    return pl.pallas_call(
        functools.partial(_fox_prep_kernel, tp=tp),
        grid=(B, nb),
        in_specs=[pl.BlockSpec((tp, LANE), lambda b, i: (b * nb + i, 0)),
                  pl.BlockSpec((1, LANE), lambda b, i: (0, 0))],
        out_specs=[pl.BlockSpec((tp, LANE), lambda b, i: (b * nb + i, 0)),
                   pl.BlockSpec((tp, MIX), lambda b, i: (b * nb + i, 0)),
                   pl.BlockSpec((tp, MIX), lambda b, i: (b * nb + i, 0))],
        out_shape=[jax.ShapeDtypeStruct((B * T, LANE), F32),
                   jax.ShapeDtypeStruct((B * T, MIX), BF16),
                   jax.ShapeDtypeStruct((B * T, MIX), BF16)],
        scratch_shapes=[pltpu.VMEM((1, LANE), F32)],
        compiler_params=_params("parallel", "arbitrary"),
        name="fox_prep",
    )(fg, bf_pad)


def _fox_attn_kernel(q_ref, qa_ref, k_ref, ka_ref, v_ref, o_ref, *, tq, tk, n_sub):
    qi = pl.program_id(2)
    sub = tq // n_sub
    qc = jnp.concatenate([q_ref[...], qa_ref[...]], axis=1)
    col = _iota2((sub, tk), 1)
    rows = [qi * tq + s * sub + _iota2((sub, tk), 0) for s in range(n_sub)]

    def body(kj, stats, masked):
        off = pl.multiple_of(kj * tk, tk)
        kc = jnp.concatenate([k_ref[pl.ds(off, tk), :], ka_ref[pl.ds(off, tk), :]], axis=1)
        vb = v_ref[pl.ds(off, tk), :]
        sc_all = lax.dot_general(qc, kc, NT_DIMS, preferred_element_type=F32)
        new = []
        for s in range(n_sub):
            m, l, acc = stats[s]
            sc = sc_all[s * sub:(s + 1) * sub, :]
            if masked:
                sc = jnp.where(col + kj * tk <= rows[s], sc, -jnp.inf)
            m_new = jnp.maximum(m, jnp.max(sc, axis=1, keepdims=True))
            alpha = jnp.exp2(m - m_new)
            p = jnp.exp2(sc - m_new)
            l = alpha * l + jnp.sum(p, axis=1, keepdims=True)
            acc = alpha * acc + jnp.dot(p.astype(BF16), vb, preferred_element_type=F32)
            new.append((m_new, l, acc))
        return tuple(new)

    n_full = (qi * tq + 1) // tk
    n_all = ((qi + 1) * tq + tk - 1) // tk
    stats = tuple((jnp.full((sub, 1), -jnp.inf, F32), jnp.zeros((sub, 1), F32), jnp.zeros((sub, HEAD_DIM), F32))
                  for _ in range(n_sub))
    stats = lax.fori_loop(0, n_full, functools.partial(body, masked=False), stats)
    stats = lax.fori_loop(n_full, n_all, functools.partial(body, masked=True), stats)
    for s in range(n_sub):
        m, l, acc = stats[s]
        o_ref[s * sub:(s + 1) * sub, :] = (acc / l).astype(BF16)


def fox_attn(q, qa, k, ka, v, B, T, tq, tk):
    nq = T // tq
    n_sub = 2 if tq >= 512 else 1
    q_spec = pl.BlockSpec((tq, HEAD_DIM), lambda b, h, i: (b * nq + i, h))
    kv_spec = pl.BlockSpec((T, HEAD_DIM), lambda b, h, i: (b, h))
    return pl.pallas_call(
        functools.partial(_fox_attn_kernel, tq=tq, tk=tk, n_sub=n_sub),
        grid=(B, N_HEADS, nq),
        in_specs=[q_spec, q_spec, kv_spec, kv_spec, kv_spec],
        out_specs=pl.BlockSpec((tq, HEAD_DIM), lambda b, h, i: (b * nq + i, h)),
        out_shape=jax.ShapeDtypeStruct((B * T, MIX), BF16),
        compiler_params=_params("parallel", "parallel", "parallel"),
        name="fox_attn",
    )(q, qa, k, ka, v)


def _suffix_sum_kernel(x_ref, o_ref):
    P = x_ref.shape[1]
    after = (_iota2((P, P), 0) > _iota2((P, P), 1)).astype(F32)
    o_ref[...] = _dot_hi(x_ref[...], after)


def suffix_sum(x):
    return pl.pallas_call(
        _suffix_sum_kernel,
        out_shape=jax.ShapeDtypeStruct(x.shape, F32),
        compiler_params=pltpu.CompilerParams(vmem_limit_bytes=VMEM_LIMIT),
        name="fox_suffix_sum",
    )(x)


def _fox_sample_kernel(q_ref, kn_ref, vn_ref, ck_ref, cv_ref, fg_ref, fgt_ref, bf_ref, bft_ref, r_ref,
                       o_ref, logf_ref, *, L):
    logf = _log_sigmoid(fg_ref[...] + bf_ref[...])
    logf_ref[...] = logf
    r_i = _iota2((L, L), 0)
    c_i = _iota2((L, L), 1)
    causal = c_i <= r_i
    ln_col = _dot_hi(causal.astype(F32), logf)
    logf_t = _log_sigmoid(fgt_ref[0] + bft_ref[...])
    ln_row = _dot_hi(logf_t, (r_i <= c_i).astype(F32))
    for h in range(N_HEADS):
        sl = slice(h * HEAD_DIM, (h + 1) * HEAD_DIM)
        q = q_ref[:, sl]
        kc = ck_ref[h].astype(BF16)
        vc = cv_ref[h].astype(BF16)
        fq = ln_col[:, h:h + 1]
        sc = lax.dot_general(q, kc, NT_DIMS, preferred_element_type=F32) + (fq + r_ref[0, h:h + 1, :])
        sn = lax.dot_general(q, kn_ref[:, sl], NT_DIMS, preferred_element_type=F32) + (fq - ln_row[h:h + 1, :])
        sn = jnp.where(causal, sn, -jnp.inf)
        m = jnp.maximum(jnp.max(sc, axis=1, keepdims=True), jnp.max(sn, axis=1, keepdims=True))
        pc = jnp.exp(sc - m)
        pn = jnp.exp(sn - m)
        l = jnp.sum(pc, axis=1, keepdims=True) + jnp.sum(pn, axis=1, keepdims=True)
        o = (jnp.dot(pc.astype(BF16), vc, preferred_element_type=F32)
             + jnp.dot(pn.astype(BF16), vn_ref[:, sl], preferred_element_type=F32)) / l
        o_ref[:, sl] = o.astype(BF16)


def fox_sample(q, kn, vn, ck, cv, layer, fg, fgt, bf_pad, bft, rsum, Bs, L):
    P = ck.shape[3]
    cache_spec = pl.BlockSpec((None, None, N_HEADS, P, HEAD_DIM), lambda b: (layer, b, 0, 0, 0))
    return pl.pallas_call(
        functools.partial(_fox_sample_kernel, L=L),
        grid=(Bs,),
        in_specs=[pl.BlockSpec((L, MIX), lambda b: (b, 0)),
                  pl.BlockSpec((L, MIX), lambda b: (b, 0)),
                  pl.BlockSpec((L, MIX), lambda b: (b, 0)),
                  cache_spec, cache_spec,
                  pl.BlockSpec((L, LANE), lambda b: (b, 0)),
                  pl.BlockSpec((1, LANE, L), lambda b: (b, 0, 0)),
                  pl.BlockSpec((1, LANE), lambda b: (0, 0)),
                  pl.BlockSpec((LANE, 1), lambda b: (0, 0)),
                  pl.BlockSpec((1, N_HEADS, P), lambda b: (b, 0, 0))],
        out_specs=[pl.BlockSpec((L, MIX), lambda b: (b, 0)),
                   pl.BlockSpec((L, LANE), lambda b: (b, 0))],
        out_shape=[jax.ShapeDtypeStruct((Bs * L, MIX), BF16),
                   jax.ShapeDtypeStruct((Bs * L, LANE), F32)],
        compiler_params=_params("parallel"),
        name="fox_sample",
    )(q, kn, vn, ck, cv, fg, fgt, bf_pad, bft, rsum)


def _gdn_prep_kernel(ab_ref, alog_ref, dtb_ref, gc_ref, beta_ref, gr_ref, *, rb, C):
    a = ab_ref[:, :LANE]
    bt = ab_ref[:, LANE:]
    g = -jnp.exp(alog_ref[...]) * _softplus(a + dtb_ref[...])
    beta_ref[...] = _sigmoid(bt)
    r_i = _iota2((rb, rb), 0)
    c_i = _iota2((rb, rb), 1)
    same_chunk_tril = jnp.logical_and(r_i // C == c_i // C, c_i <= r_i).astype(F32)
    G = _dot_hi(same_chunk_tril, g)
    gc_ref[...] = G
    eye = (_iota2((LANE, LANE), 0) == _iota2((LANE, LANE), 1)).astype(F32)
    gr_ref[0, 0] = _dot_nt_hi(eye, G)


def gdn_prep(ab, alog_pad, dtb_pad, B, T, rb, C):
    nb = T // rb
    return pl.pallas_call(
        functools.partial(_gdn_prep_kernel, rb=rb, C=C),
        grid=(B, nb),
        in_specs=[pl.BlockSpec((rb, 2 * LANE), lambda b, i: (b * nb + i, 0)),
                  pl.BlockSpec((1, LANE), lambda b, i: (0, 0)),
                  pl.BlockSpec((1, LANE), lambda b, i: (0, 0))],
        out_specs=[pl.BlockSpec((rb, LANE), lambda b, i: (b * nb + i, 0)),
                   pl.BlockSpec((rb, LANE), lambda b, i: (b * nb + i, 0)),
                   pl.BlockSpec((1, 1, LANE, rb), lambda b, i: (b, i, 0, 0))],
        out_shape=[jax.ShapeDtypeStruct((B * T, LANE), F32),
                   jax.ShapeDtypeStruct((B * T, LANE), F32),
                   jax.ShapeDtypeStruct((B, nb, LANE, rb), F32)],
        compiler_params=_params("parallel", "parallel"),
        name="gdn_prep",
    )(ab, alog_pad, dtb_pad)


def _gdn_kernel(xq_ref, xk_ref, xv_ref, wq_ref, wk_ref, wv_ref, gc_ref, beta_ref, gr_ref, s0_ref,
                bq_ref, bk_ref, bv_ref, gain_ref,
                o_ref, sout_ref, cq_ref, ck_ref, cv_ref, S_ref, tail_ref, *, rb, C, nblk, hb):
    hblk = pl.program_id(1)
    i = pl.program_id(2)
    nc = rb // C
    W = hb * HEAD_DIM
    n_iter = C.bit_length() - 2

    @pl.when(i == 0)
    def _():
        S_ref[...] = s0_ref[...]
        tail_ref[...] = jnp.zeros_like(tail_ref)
        tail_ref[0, 5:8, :] = bq_ref[...]
        tail_ref[1, 5:8, :] = bk_ref[...]
        tail_ref[2, 5:8, :] = bv_ref[...]

    row8 = _iota2((8, W), 0)

    def conv_silu(x_ref, w_ref, idx):
        x = x_ref[...]
        w = w_ref[...]
        t8 = tail_ref[idx]
        y = x * w[3:4, :]
        for j in range(1, 4):
            xr = pltpu.roll(x, j, 0)
            head = jnp.where(row8 < j, pltpu.roll(t8, j, 0), xr[:8])
            xs = jnp.concatenate([head, xr[8:]], axis=0)
            y = y + xs * w[3 - j:4 - j, :]
        tail_ref[idx] = x[rb - 8:, :]
        return y * _sigmoid(y)

    qc_all = conv_silu(xq_ref, wq_ref, 0)
    kc_all = conv_silu(xk_ref, wk_ref, 1)
    vv_all = conv_silu(xv_ref, wv_ref, 2)

    r_i = _iota2((C, C), 0)
    c_i = _iota2((C, C), 1)
    incl = c_i <= r_i
    strict = c_i < r_i
    lane = _iota2((rb, LANE), 1)
    gc_blk = gc_ref[...]
    beta_blk = beta_ref[...]

    heads = []
    for hh in range(hb):
        hsl = slice(hh * HEAD_DIM, (hh + 1) * HEAD_DIM)
        qc = qc_all[:, hsl]
        kc = kc_all[:, hsl]
        qn = qc * lax.rsqrt(jnp.sum(qc * qc, axis=-1, keepdims=True) + EPS) * (HEAD_DIM ** -0.5)
        kn = kc * lax.rsqrt(jnp.sum(kc * kc, axis=-1, keepdims=True) + EPS)
        head_id = hblk * hb + hh
        sel = lane == head_id
        g_col = jnp.sum(jnp.where(sel, gc_blk, 0.0), axis=1, keepdims=True)
        b_col = jnp.sum(jnp.where(sel, beta_blk, 0.0), axis=1, keepdims=True)
        g_row = gr_ref[0, 0, pl.ds(head_id, 1), :]
        heads.append(dict(qn=qn, kn=kn, vv=vv_all[:, hsl], g_col=g_col, b_col=b_col, g_row=g_row,
                          eg_col=jnp.exp(g_col), S=S_ref[hh], outs=[]))

    units = []
    for c in range(nc):
        sl = slice(c * C, (c + 1) * C)
        for hd in heads:
            gi = hd["g_col"][sl]
            u_ = dict(hd=hd, gi=gi, bi=hd["b_col"][sl], egi=hd["eg_col"][sl],
                      k=hd["kn"][sl], q=hd["qn"][sl], v=hd["vv"][sl])
            u_["decay"] = jnp.exp(jnp.where(incl, gi - hd["g_row"][:, sl], -jnp.inf))
            units.append(u_)
    for u_ in units:
        u_["kq"] = _mm1(jnp.concatenate([u_["k"], u_["q"]], axis=0), u_["k"], NT_DIMS)
    for u_ in units:
        u_["P"] = -jnp.where(strict, u_["bi"] * u_["kq"][:C] * u_["decay"], 0.0)
        u_["qk"] = jnp.where(incl, u_["kq"][C:] * u_["decay"], 0.0)
        u_["X"] = jnp.concatenate([u_["v"] * u_["bi"], u_["k"] * (u_["bi"] * u_["egi"])], axis=1)
    for _ in range(n_iter):
        for u_ in units:
            u_["Y"] = _mm2r(u_["P"], jnp.concatenate([u_["X"], u_["P"]], axis=1))
        for u_ in units:
            u_["X"] = u_["X"] + u_["Y"][:, :2 * HEAD_DIM]
            u_["P"] = u_["Y"][:, 2 * HEAD_DIM:]
    for u_ in units:
        u_["Y"] = _mm2r(u_["P"], u_["X"])
    for u_ in units:
        X = u_["X"] + u_["Y"]
        g_last = u_["gi"][C - 1:C, :]
        u_["u"] = X[:, :HEAD_DIM]
        u_["wq"] = jnp.concatenate([X[:, HEAD_DIM:], u_["q"] * u_["egi"]], axis=0)
        u_["qk_kt"] = jnp.concatenate([u_["qk"], (u_["k"] * jnp.exp(g_last - u_["gi"])).T], axis=0)
        u_["g_last"] = jnp.exp(g_last)

    for c in range(nc):
        cu = units[c * hb:(c + 1) * hb]
        for u_ in cu:
            u_["ws_qs"] = _mm1(u_["wq"], u_["hd"]["S"])
        for u_ in cu:
            u_["v_new"] = u_["u"] - u_["ws_qs"][:C]
            u_["od"] = _mm2r(u_["qk_kt"], u_["v_new"])
        for u_ in cu:
            hd = u_["hd"]
            hd["outs"].append(u_["ws_qs"][C:] + u_["od"][:C])
            hd["S"] = hd["S"] * u_["g_last"] + u_["od"][C:]

    gain = gain_ref[...]
    for hh, hd in enumerate(heads):
        hsl = slice(hh * HEAD_DIM, (hh + 1) * HEAD_DIM)
        S_ref[hh] = hd["S"]
        o = hd["outs"][0] if nc == 1 else jnp.concatenate(hd["outs"], axis=0)
        y = o * lax.rsqrt(jnp.mean(o * o, axis=-1, keepdims=True) + EPS) * gain
        o_ref[:, hsl] = y.astype(BF16)

    @pl.when(i == nblk - 1)
    def _():
        for hh, hd in enumerate(heads):
            sout_ref[0, hh] = hd["S"]
        cq_ref[0] = xq_ref[pl.ds(rb - 3, 3), :]
        ck_ref[0] = xk_ref[pl.ds(rb - 3, 3), :]
        cv_ref[0] = xv_ref[pl.ds(rb - 3, 3), :]


def gdn_mixer(qkv, conv_w, gc, beta, gr, S0, buf0, layer, out_gain, B, T, rb, C, hb):
    nblk = T // rb
    H = N_HEADS
    W = hb * HEAD_DIM
    nhb = H // hb
    x_spec = lambda off: pl.BlockSpec((rb, W), lambda b, h, i: (b * nblk + i, off + h))
    w_spec = lambda off: pl.BlockSpec((4, W), lambda b, h, i: (0, off + h))
    b_spec = lambda off: pl.BlockSpec((None, None, 3, W), lambda b, h, i: (layer, b, 0, off + h))
    col_spec = pl.BlockSpec((rb, LANE), lambda b, h, i: (b * nblk + i, 0))
    s_spec = pl.BlockSpec((1, hb, HEAD_DIM, HEAD_DIM), lambda b, h, i: (b, h, 0, 0))
    c_out = pl.BlockSpec((1, 3, W), lambda b, h, i: (b, 0, h))
    mix, S, cq, ck, cv = pl.pallas_call(
        functools.partial(_gdn_kernel, rb=rb, C=C, nblk=nblk, hb=hb),
        grid=(B, nhb, nblk),
        in_specs=[x_spec(0), x_spec(nhb), x_spec(2 * nhb), w_spec(0), w_spec(nhb), w_spec(2 * nhb),
                  col_spec, col_spec,
                  pl.BlockSpec((1, 1, LANE, rb), lambda b, h, i: (b, i, 0, 0)),
                  pl.BlockSpec((None, None, hb, HEAD_DIM, HEAD_DIM), lambda b, h, i: (layer, b, h, 0, 0)),
                  b_spec(0), b_spec(nhb), b_spec(2 * nhb),
                  pl.BlockSpec((1, HEAD_DIM), lambda b, h, i: (0, 0))],
        out_specs=[pl.BlockSpec((rb, W), lambda b, h, i: (b * nblk + i, h)),
                   s_spec, c_out, c_out, c_out],
        out_shape=[jax.ShapeDtypeStruct((B * T, MIX), BF16),
                   jax.ShapeDtypeStruct((B, H, HEAD_DIM, HEAD_DIM), F32),
                   jax.ShapeDtypeStruct((B, 3, MIX), F32),
                   jax.ShapeDtypeStruct((B, 3, MIX), F32),
                   jax.ShapeDtypeStruct((B, 3, MIX), F32)],
        scratch_shapes=[pltpu.VMEM((hb, HEAD_DIM, HEAD_DIM), F32),
                        pltpu.VMEM((3, 8, W), F32)],
        compiler_params=_params("parallel", "parallel", "arbitrary"),
        name="gdn_mixer",
    )(qkv, qkv, qkv, conv_w, conv_w, conv_w, gc, beta, gr, S0, buf0, buf0, buf0, out_gain.reshape(1, HEAD_DIM))
    return mix, S, jnp.concatenate([cq, ck, cv], axis=-1)


def _pad_cols(w, n):
    return jnp.pad(w, ((0, 0), (0, n - w.shape[1])))


def _pad_vec(v, n):
    return jnp.pad(v.astype(F32), (0, n - v.shape[0])).reshape(1, n)


def kernel(x_prompt, x_sample, mem_prompt, cache_fox_k, cache_fox_v, cache_fox_logf, state_gdn_S, state_gdn_conv, cache_mem_k, cache_mem_v, norm_gain, mem_norm_gain, w_mem_kv, w_in_fox, b_forget, w_in_gdn, gdn_conv_w, gdn_A_log, gdn_dt_bias, gdn_out_norm, w_out, final_norm):
    B, T, D = x_prompt.shape
    Bs, L, _ = x_sample.shape
    depth = norm_gain.shape[0]
    NM = mem_prompt.shape[1]
    P = cache_fox_k.shape[2]
    H = N_HEADS
    scale = HEAD_DIM ** -0.5
    QKV = 3 * MIX

    xp = x_prompt.reshape(B * T, D)
    xs = x_sample.reshape(Bs * L, D)
    mem = mem_prompt.reshape(B * NM, D)

    tq = min(T, 512)
    tk = min(T, 1024)
    n_fox = (depth + 1) // 2
    pk_all = pv_all = sk_all = sv_all = None
    rb_p = min(T, 4 * CHUNK)
    head_major = (0, 1, 3, 2, 4)
    ck_heads = jnp.transpose(cache_fox_k, head_major)
    cv_heads = jnp.transpose(cache_fox_v, head_major)

    p_lf, p_S, p_conv, p_mk, p_mv = [], [], [], [], []
    s_lf, s_S, s_conv = [], [], []

    for i in range(depth):
        j = i // 2
        hm = rms_norm_rows(mem, mem_norm_gain[i], BF16)
        wkv = w_mem_kv[i].astype(BF16)
        mk_p = mm(hm, wkv[:, :MEMW], "f32").reshape(B, NM, MEMW)
        mv_p = mm(hm, wkv[:, MEMW:], "f32").reshape(B, NM, MEMW)
        p_mk.append(mk_p)
        p_mv.append(mv_p)

        if i == 0:
            hp = rms_norm_rows(xp, norm_gain[0], BF16)
            hs = rms_norm_rows(xs, norm_gain[0], BF16)
        w_o = w_out[i].astype(BF16)

        if i % 2 == 0:
            w = w_in_fox[j]
            wq = w[:, :MIX].astype(BF16)
            wk = w[:, MIX:2 * MIX].astype(BF16)
            wv = w[:, 2 * MIX:QKV].astype(BF16)
            wf = _pad_cols(w[:, QKV:QKV + H], LANE).astype(BF16)
            wqm = w[:, QKV + H:QKV + H + MEMW].astype(BF16)
            wz = w[:, QKV + H + MEMW:].astype(BF16)
            bf_pad = _pad_vec(b_forget[j], LANE)

            q = mm(hp, wq, "scale_bf16", scale * LOG2E)
            pk_all, k16 = proj_kv(hp, wk, pk_all, j, n_fox, B, T)
            pv_all, v16 = proj_kv(hp, wv, pv_all, j, n_fox, B, T)
            fg = mm(hp, wf, "f32")
            qm_p = mm(hp, wqm, "scale_bf16", scale)
            g_p = mm(hp, wz, "silu_bf16")
            logf, qa, ka = fox_prep(fg, bf_pad, B, T, min(T, 512))
            mix_p = fox_attn(q, qa, k16, ka, v16, B, T, tq, tk)
            p_lf.append(logf[:, :H].reshape(B, T, H))

            q = mm(hs, wq, "scale_bf16", scale)
            sk_all, k16 = proj_kv(hs, wk, sk_all, j, n_fox, Bs, L)
            sv_all, v16 = proj_kv(hs, wv, sv_all, j, n_fox, Bs, L)
            fg = mm(hs, wf, "f32")
            qm_s = mm(hs, wqm, "scale_bf16", scale)
            g_s = mm(hs, wz, "silu_bf16")
            fgt = fg.reshape(Bs, L, LANE).transpose(0, 2, 1)
            clf_t = cache_fox_logf[j].astype(F32).transpose(0, 2, 1).reshape(Bs * H, P)
            rsum = suffix_sum(clf_t).reshape(Bs, H, P)
            mix_s, logf_s = fox_sample(q, k16, v16, ck_heads, cv_heads, j,
                                       fg, fgt, bf_pad, bf_pad.reshape(LANE, 1), rsum, Bs, L)
            s_lf.append(logf_s[:, :H].reshape(Bs, L, H))
        else:
            w = w_in_gdn[j]
            wqkv = w[:, :QKV].astype(BF16)
            wab = jnp.concatenate([_pad_cols(w[:, QKV:QKV + H], LANE),
                                   _pad_cols(w[:, QKV + H:QKV + 2 * H], LANE)], axis=1).astype(BF16)
            wqm = w[:, QKV + 2 * H:QKV + 2 * H + MEMW].astype(BF16)
            wz = w[:, QKV + 2 * H + MEMW:].astype(BF16)
            alog_pad = _pad_vec(gdn_A_log[j], LANE)
            dtb_pad = _pad_vec(gdn_dt_bias[j], LANE)

            qkv = mm(hp, wqkv, "f32")
            ab = mm(hp, wab, "f32")
            qm_p = mm(hp, wqm, "scale_bf16", scale)
            g_p = mm(hp, wz, "silu_bf16")
            gc, beta, gr = gdn_prep(ab, alog_pad, dtb_pad, B, T, rb_p, min(CHUNK, T))
            mix_p, S, conv = gdn_mixer(qkv, gdn_conv_w[j], gc, beta, gr,
                                       jnp.zeros((1, B, H, HEAD_DIM, HEAD_DIM), F32),
                                       jnp.zeros((1, B, 3, QKV), F32), 0,
                                       gdn_out_norm[j], B, T, rb_p, min(CHUNK, T), 2)
            p_S.append(S)
            p_conv.append(conv)

            qkv = mm(hs, wqkv, "f32")
            ab = mm(hs, wab, "f32")
            qm_s = mm(hs, wqm, "scale_bf16", scale)
            g_s = mm(hs, wz, "silu_bf16")
            gc, beta, gr = gdn_prep(ab, alog_pad, dtb_pad, Bs, L, L, L)
            mix_s, S, conv = gdn_mixer(qkv, gdn_conv_w[j], gc, beta, gr, state_gdn_S, state_gdn_conv, j,
                                       gdn_out_norm[j], Bs, L, L, L, H)
            s_S.append(S)
            s_conv.append(conv)

        m_p = mem_attn(qm_p, mk_p, mv_p, T)
        m_s = mem_attn(qm_s, cache_mem_k, cache_mem_v, L, layer=i)
        if i + 1 < depth:
            xp, hp = out_proj(mix_p, m_p, g_p, w_o, xp, norm_gain[i + 1], False)
            xs, hs = out_proj(mix_s, m_s, g_s, w_o, xs, norm_gain[i + 1], False)
        else:
            y_prompt = out_proj(mix_p, m_p, g_p, w_o, xp, final_norm, True).reshape(B, T, D)
            y_sample = out_proj(mix_s, m_s, g_s, w_o, xs, final_norm, True).reshape(Bs, L, D)

    mem_shape = (depth, B, NM, N_MEM_HEADS, HEAD_DIM)
    return (y_prompt, y_sample,
            jnp.transpose(pk_all, head_major), jnp.transpose(pv_all, head_major), jnp.stack(p_lf), jnp.stack(p_S), jnp.stack(p_conv),
            jnp.stack(p_mk).reshape(mem_shape), jnp.stack(p_mv).reshape(mem_shape),
            jnp.transpose(sk_all, head_major), jnp.transpose(sv_all, head_major), jnp.stack(s_lf), jnp.stack(s_S), jnp.stack(s_conv))
```

```python
import functools
import math

import jax
import jax.numpy as jnp
from jax import lax
from jax.experimental import pallas as pl
from jax.experimental.pallas import tpu as pltpu

F32 = jnp.float32
BF16 = jnp.bfloat16
HI = lax.Precision.HIGHEST

LANE = 128
HEAD_DIM = 128
N_HEADS = 12
MIX = N_HEADS * HEAD_DIM
N_MEM_HEADS = 4
MEMW = N_MEM_HEADS * HEAD_DIM
CHUNK = 64
EPS = 1e-6
LOG2E = math.log2(math.e)
VMEM_LIMIT = 48 * 1024 * 1024

NN_DIMS = (((1,), (0,)), ((), ()))
NT_DIMS = (((1,), (1,)), ((), ()))


def _params(*sem):
    return pltpu.CompilerParams(dimension_semantics=sem, vmem_limit_bytes=VMEM_LIMIT)


def _sigmoid(x):
    return 1.0 / (1.0 + jnp.exp(-x))


def _log_sigmoid(x):
    return jnp.minimum(x, 0.0) - jnp.log1p(jnp.exp(-jnp.abs(x)))


def _softplus(x):
    return jnp.maximum(x, 0.0) + jnp.log1p(jnp.exp(-jnp.abs(x)))


def _dot_hi(a, b):
    return jnp.dot(a, b, precision=HI, preferred_element_type=F32)


def _dot_nt_hi(a, b):
    return lax.dot_general(a, b, NT_DIMS, precision=HI, preferred_element_type=F32)


def _split_bf16(a):
    hi = a.astype(BF16)
    lo = (a - hi.astype(F32)).astype(BF16)
    return hi, lo


def _mm1(a, b, dims=NN_DIMS):
    return lax.dot_general(a.astype(BF16), b.astype(BF16), dims, preferred_element_type=F32)


def _mm2r(a, b):
    b_hi, b_lo = _split_bf16(b)
    aa = a.astype(BF16)
    return (jnp.dot(aa, b_hi, preferred_element_type=F32)
            + jnp.dot(aa, b_lo, preferred_element_type=F32))


def _iota2(shape, dim):
    return lax.broadcasted_iota(jnp.int32, shape, dim)


def _rms_kernel(x_ref, g_ref, o_ref):
    x = x_ref[...]
    y = x * lax.rsqrt(jnp.mean(x * x, axis=-1, keepdims=True) + EPS)
    o_ref[...] = (y * g_ref[...]).astype(o_ref.dtype)


def rms_norm_rows(x, gain, out_dtype):
    R, D = x.shape
    tm = min(R, 512)
    return pl.pallas_call(
        _rms_kernel,
        grid=(R // tm,),
        in_specs=[pl.BlockSpec((tm, D), lambda i: (i, 0)),
                  pl.BlockSpec((1, D), lambda i: (0, 0))],
        out_specs=pl.BlockSpec((tm, D), lambda i: (i, 0)),
        out_shape=jax.ShapeDtypeStruct((R, D), out_dtype),
        compiler_params=_params("parallel"),
        name="rms_norm",
    )(x, gain.reshape(1, D))


def _cast_cols_kernel(w_ref, *o_refs, cols):
    for o_ref, (start, width) in zip(o_refs, cols):
        x = w_ref[:, start:start + width].astype(BF16)
        pad = o_ref.shape[1] - width
        if pad:
            x = jnp.concatenate([x, jnp.zeros((x.shape[0], pad), BF16)], axis=1)
        o_ref[...] = x


def cast_cols(w_all, layer, cols):
    _, K, N = w_all.shape
    tkb = min(K, 256)
    widths = [-(-width // LANE) * LANE for _, width in cols]
    return pl.pallas_call(
        functools.partial(_cast_cols_kernel, cols=tuple(cols)),
        grid=(K // tkb,),
        in_specs=[pl.BlockSpec((None, tkb, N), lambda i: (layer, i, 0))],
        out_specs=[pl.BlockSpec((tkb, wd), lambda i: (i, 0)) for wd in widths],
        out_shape=[jax.ShapeDtypeStruct((K, wd), BF16) for wd in widths],
        compiler_params=_params("parallel"),
        name="cast_cols",
    )(w_all)


def _mm_kernel(h_ref, w_ref, *o_refs, kind, scale):
    acc = jnp.dot(h_ref[...], w_ref[...], preferred_element_type=F32)
    if kind == "f32":
        o_refs[0][...] = acc
    elif kind == "scale_bf16":
        o_refs[0][...] = (acc * scale).astype(BF16)
    elif kind == "silu_bf16":
        o_refs[0][...] = (acc * _sigmoid(acc)).astype(BF16)
    else:
        raise ValueError(kind)


def mm(h, w, kind, scale=1.0):
    R, K = h.shape
    N = w.shape[1]
    tm = min(R, 1024)
    tn = min(N, 512)
    out_dtypes = {"f32": [F32], "scale_bf16": [BF16], "silu_bf16": [BF16]}[kind]
    outs = pl.pallas_call(
        functools.partial(_mm_kernel, kind=kind, scale=scale),
        grid=(R // tm, N // tn),
        in_specs=[pl.BlockSpec((tm, K), lambda i, n: (i, 0)),
                  pl.BlockSpec((K, tn), lambda i, n: (0, n))],
        out_specs=[pl.BlockSpec((tm, tn), lambda i, n: (i, n)) for _ in out_dtypes],
        out_shape=[jax.ShapeDtypeStruct((R, N), dt) for dt in out_dtypes],
        compiler_params=_params("parallel", "arbitrary"),
        name="proj_" + kind,
    )(h, w)
    return outs if len(outs) > 1 else outs[0]


def _proj_kv_kernel(h_ref, w_ref, *refs):
    o32_ref, o16_ref = refs[-2], refs[-1]
    nb, _, tt, _ = o32_ref.shape
    acc = jnp.dot(h_ref[...], w_ref[...], preferred_element_type=F32)
    o16_ref[...] = acc.astype(BF16)
    o32_ref[:, 0] = acc[:, :HEAD_DIM].reshape(nb, tt, HEAD_DIM)
    o32_ref[:, 1] = acc[:, HEAD_DIM:].reshape(nb, tt, HEAD_DIM)


def proj_kv(h, w, stacked, layer, n_layers, NB, T):
    R, K = h.shape
    tm = min(R, 1024)
    tt = min(tm, T)
    nb = tm // tt
    nt = T // tt
    in_specs = [pl.BlockSpec((tm, K), lambda i, hp: (i, 0)),
                pl.BlockSpec((K, 2 * HEAD_DIM), lambda i, hp: (0, hp))]
    args = [h, w]
    aliases = {}
    if stacked is not None:
        in_specs.append(pl.BlockSpec(memory_space=pl.ANY))
        args.append(stacked)
        aliases = {2: 0}
    return pl.pallas_call(
        _proj_kv_kernel,
        grid=(R // tm, N_HEADS // 2),
        in_specs=in_specs,
        out_specs=[pl.BlockSpec((None, nb, 2, tt, HEAD_DIM), lambda i, hp: (layer, i // nt, hp, i % nt, 0)),
                   pl.BlockSpec((tm, 2 * HEAD_DIM), lambda i, hp: (i, hp))],
        out_shape=[jax.ShapeDtypeStruct((n_layers, NB, N_HEADS, T, HEAD_DIM), F32),
                   jax.ShapeDtypeStruct((R, MIX), BF16)],
        input_output_aliases=aliases,
        compiler_params=_params("parallel", "arbitrary"),
        name="proj_kv",
    )(*args)


def _out_proj_kernel(mix_ref, m_ref, g_ref, w_ref, x_ref, gain_ref, *o_refs):
    g = g_ref[...].astype(F32)
    br = jnp.concatenate([(mix_ref[...].astype(F32) * g[:, :MIX]).astype(BF16),
                          (m_ref[...].astype(F32) * g[:, MIX:]).astype(BF16)], axis=1)
    x = x_ref[...] + jnp.dot(br, w_ref[...], preferred_element_type=F32)
    y = x * lax.rsqrt(jnp.mean(x * x, axis=-1, keepdims=True) + EPS) * gain_ref[...]
    o_refs[-1][...] = y.astype(o_refs[-1].dtype)
    if len(o_refs) == 2:
        o_refs[0][...] = x


def out_proj(mix, m, g, w, x, gain, last):
    R, D = x.shape
    BW = MIX + MEMW
    tm = min(R, 256)
    row = lambda width: pl.BlockSpec((tm, width), lambda i: (i, 0))
    if last:
        out_specs, out_shape = [row(D)], [jax.ShapeDtypeStruct((R, D), F32)]
    else:
        out_specs = [row(D), row(D)]
        out_shape = [jax.ShapeDtypeStruct((R, D), F32), jax.ShapeDtypeStruct((R, D), BF16)]
    outs = pl.pallas_call(
        _out_proj_kernel,
        grid=(R // tm,),
        in_specs=[row(MIX), row(MEMW), row(BW),
                  pl.BlockSpec((BW, D), lambda i: (0, 0)),
                  row(D),
                  pl.BlockSpec((1, D), lambda i: (0, 0))],
        out_specs=out_specs,
        out_shape=out_shape,
        compiler_params=_params("parallel"),
        name="out_proj",
    )(mix, m, g, w, x, gain.reshape(1, D))
    return outs[0] if last else outs


def _mem_attn_kernel(q_ref, k_ref, v_ref, o_ref, *, per_head_kv):
    for h in range(N_MEM_HEADS):
        sl = slice(h * HEAD_DIM, (h + 1) * HEAD_DIM)
        q = q_ref[:, sl]
        if per_head_kv:
            k = k_ref[:, h, :].astype(BF16)
            v = v_ref[:, h, :].astype(BF16)
        else:
            k = k_ref[:, sl].astype(BF16)
            v = v_ref[:, sl].astype(BF16)
        s = lax.dot_general(q, k, NT_DIMS, preferred_element_type=F32)
        m = jnp.max(s, axis=1, keepdims=True)
        p = jnp.exp(s - m)
        l = jnp.sum(p, axis=1, keepdims=True)
        o = jnp.dot(p.astype(BF16), v, preferred_element_type=F32) / l
        o_ref[:, sl] = o.astype(BF16)


def mem_attn(q, mk, mv, rows_per_batch, layer=None):
    R = q.shape[0]
    per_head_kv = layer is not None
    tr = min(rows_per_batch, 1024)
    nrb = rows_per_batch // tr
    if per_head_kv:
        NB, NM = mk.shape[1:3]
        kv_spec = pl.BlockSpec((None, None, NM, N_MEM_HEADS, HEAD_DIM), lambda b, i: (layer, b, 0, 0, 0))
    else:
        NB, NM = mk.shape[:2]
        kv_spec = pl.BlockSpec((None, NM, MEMW), lambda b, i: (b, 0, 0))
    return pl.pallas_call(
        functools.partial(_mem_attn_kernel, per_head_kv=per_head_kv),
        grid=(NB, nrb),
        in_specs=[pl.BlockSpec((tr, MEMW), lambda b, i: (b * nrb + i, 0)), kv_spec, kv_spec],
        out_specs=pl.BlockSpec((tr, MEMW), lambda b, i: (b * nrb + i, 0)),
        out_shape=jax.ShapeDtypeStruct((R, MEMW), BF16),
        compiler_params=_params("parallel", "parallel"),
        name="mem_attn",
    )(q, mk, mv)


def _fox_prep_kernel(fg_ref, bf_ref, logf_ref, qa_ref, ka_ref, carry_ref, *, tp):
    @pl.when(pl.program_id(1) == 0)
    def _():
        carry_ref[...] = jnp.zeros_like(carry_ref)

    logf = _log_sigmoid(fg_ref[...] + bf_ref[...])
    tril = (_iota2((tp, tp), 1) <= _iota2((tp, tp), 0)).astype(F32)
    F = _dot_hi(tril, logf) + carry_ref[...]
    carry_ref[...] = F[tp - 1:tp, :]
    logf_ref[...] = logf

    F2 = F * LOG2E
    hi = F2.astype(BF16)
    r1 = F2 - hi.astype(F32)
    mid = r1.astype(BF16)
    lo = (r1 - mid.astype(F32)).astype(BF16)
    col = _iota2((LANE, MIX), 1)
    own_head = _iota2((LANE, MIX), 0) == (col >> 7)
    slot = col & (LANE - 1)

    def spread(x, c):
        sel = jnp.logical_and(own_head, slot == c).astype(BF16)
        return jnp.dot(x, sel, preferred_element_type=F32)

    slot_row = _iota2((1, MIX), 1) & (LANE - 1)
    ones_q = jnp.logical_and(slot_row >= 3, slot_row < 6).astype(F32)
    ones_k = (slot_row < 3).astype(F32)
    qa_ref[...] = (spread(hi, 0) + spread(mid, 1) + spread(lo, 2) + ones_q).astype(BF16)
    ka_ref[...] = (ones_k - (spread(hi, 3) + spread(mid, 4) + spread(lo, 5))).astype(BF16)


def fox_prep(fg, bf_pad, B, T, tp):
    nb = T // tp
    return pl.pallas_call(
        functools.partial(_fox_prep_kernel, tp=tp),
        grid=(B, nb),
        in_specs=[pl.BlockSpec((tp, LANE), lambda b, i: (b * nb + i, 0)),
                  pl.BlockSpec((1, LANE), lambda b, i: (0, 0))],
        out_specs=[pl.BlockSpec((tp, LANE), lambda b, i: (b * nb + i, 0)),
                   pl.BlockSpec((tp, MIX), lambda b, i: (b * nb + i, 0)),
                   pl.BlockSpec((tp, MIX), lambda b, i: (b * nb + i, 0))],
        out_shape=[jax.ShapeDtypeStruct((B * T, LANE), F32),
                   jax.ShapeDtypeStruct((B * T, MIX), BF16),
                   jax.ShapeDtypeStruct((B * T, MIX), BF16)],
        scratch_shapes=[pltpu.VMEM((1, LANE), F32)],
        compiler_params=_params("parallel", "arbitrary"),
        name="fox_prep",
    )(fg, bf_pad)


def _fox_attn_kernel(q_ref, qa_ref, k_ref, ka_ref, v_ref, o_ref, *, tq, tk, n_sub):
    qi = pl.program_id(2)
    sub = tq // n_sub
    qc = jnp.concatenate([q_ref[...], qa_ref[...]], axis=1)
    col = _iota2((sub, tk), 1)
    rows = [qi * tq + s * sub + _iota2((sub, tk), 0) for s in range(n_sub)]

    def body(kj, stats, masked):
        off = pl.multiple_of(kj * tk, tk)
        kc = jnp.concatenate([k_ref[pl.ds(off, tk), :], ka_ref[pl.ds(off, tk), :]], axis=1)
        vb = v_ref[pl.ds(off, tk), :]
        sc_all = lax.dot_general(qc, kc, NT_DIMS, preferred_element_type=F32)
        new = []
        for s in range(n_sub):
            m, l, acc = stats[s]
            sc = sc_all[s * sub:(s + 1) * sub, :]
            if masked:
                sc = jnp.where(col + kj * tk <= rows[s], sc, -jnp.inf)
            m_new = jnp.maximum(m, jnp.max(sc, axis=1, keepdims=True))
            alpha = jnp.exp2(m - m_new)
            p = jnp.exp2(sc - m_new)
            l = alpha * l + jnp.sum(p, axis=1, keepdims=True)
            acc = alpha * acc + jnp.dot(p.astype(BF16), vb, preferred_element_type=F32)
            new.append((m_new, l, acc))
        return tuple(new)

    n_full = (qi * tq + 1) // tk
    n_all = ((qi + 1) * tq + tk - 1) // tk
    stats = tuple((jnp.full((sub, 1), -jnp.inf, F32), jnp.zeros((sub, 1), F32), jnp.zeros((sub, HEAD_DIM), F32))
                  for _ in range(n_sub))
    stats = lax.fori_loop(0, n_full, functools.partial(body, masked=False), stats)
    stats = lax.fori_loop(n_full, n_all, functools.partial(body, masked=True), stats)
    for s in range(n_sub):
        m, l, acc = stats[s]
        o_ref[s * sub:(s + 1) * sub, :] = (acc / l).astype(BF16)


def fox_attn(q, qa, k, ka, v, B, T, tq, tk):
    nq = T // tq
    n_sub = 2 if tq >= 512 else 1
    q_spec = pl.BlockSpec((tq, HEAD_DIM), lambda b, h, i: (b * nq + i, h))
    kv_spec = pl.BlockSpec((T, HEAD_DIM), lambda b, h, i: (b, h))
    return pl.pallas_call(
        functools.partial(_fox_attn_kernel, tq=tq, tk=tk, n_sub=n_sub),
        grid=(B, N_HEADS, nq),
        in_specs=[q_spec, q_spec, kv_spec, kv_spec, kv_spec],
        out_specs=pl.BlockSpec((tq, HEAD_DIM), lambda b, h, i: (b * nq + i, h)),
        out_shape=jax.ShapeDtypeStruct((B * T, MIX), BF16),
        compiler_params=_params("parallel", "parallel", "parallel"),
        name="fox_attn",
    )(q, qa, k, ka, v)


def _suffix_sum_kernel(x_ref, o_ref):
    P = x_ref.shape[1]
    after = (_iota2((P, P), 0) > _iota2((P, P), 1)).astype(F32)
    o_ref[...] = _dot_hi(x_ref[...], after)


def suffix_sum(x):
    return pl.pallas_call(
        _suffix_sum_kernel,
        out_shape=jax.ShapeDtypeStruct(x.shape, F32),
        compiler_params=pltpu.CompilerParams(vmem_limit_bytes=VMEM_LIMIT),
        name="fox_suffix_sum",
    )(x)


def _fox_sample_kernel(q_ref, kn_ref, vn_ref, ck_ref, cv_ref, fg_ref, fgt_ref, bf_ref, bft_ref, r_ref,
                       o_ref, logf_ref, *, L):
    logf = _log_sigmoid(fg_ref[...] + bf_ref[...])
    logf_ref[...] = logf
    r_i = _iota2((L, L), 0)
    c_i = _iota2((L, L), 1)
    causal = c_i <= r_i
    ln_col = _dot_hi(causal.astype(F32), logf)
    logf_t = _log_sigmoid(fgt_ref[0] + bft_ref[...])
    ln_row = _dot_hi(logf_t, (r_i <= c_i).astype(F32))
    for h in range(N_HEADS):
        sl = slice(h * HEAD_DIM, (h + 1) * HEAD_DIM)
        q = q_ref[:, sl]
        kc = ck_ref[h].astype(BF16)
        vc = cv_ref[h].astype(BF16)
        fq = ln_col[:, h:h + 1]
        sc = lax.dot_general(q, kc, NT_DIMS, preferred_element_type=F32) + (fq + r_ref[0, h:h + 1, :])
        sn = lax.dot_general(q, kn_ref[:, sl], NT_DIMS, preferred_element_type=F32) + (fq - ln_row[h:h + 1, :])
        sn = jnp.where(causal, sn, -jnp.inf)
        m = jnp.maximum(jnp.max(sc, axis=1, keepdims=True), jnp.max(sn, axis=1, keepdims=True))
        pc = jnp.exp(sc - m)
        pn = jnp.exp(sn - m)
        l = jnp.sum(pc, axis=1, keepdims=True) + jnp.sum(pn, axis=1, keepdims=True)
        o = (jnp.dot(pc.astype(BF16), vc, preferred_element_type=F32)
             + jnp.dot(pn.astype(BF16), vn_ref[:, sl], preferred_element_type=F32)) / l
        o_ref[:, sl] = o.astype(BF16)


def fox_sample(q, kn, vn, ck, cv, layer, fg, fgt, bf_pad, bft, rsum, Bs, L):
    P = ck.shape[3]
    cache_spec = pl.BlockSpec((None, None, N_HEADS, P, HEAD_DIM), lambda b: (layer, b, 0, 0, 0))
    return pl.pallas_call(
        functools.partial(_fox_sample_kernel, L=L),
        grid=(Bs,),
        in_specs=[pl.BlockSpec((L, MIX), lambda b: (b, 0)),
                  pl.BlockSpec((L, MIX), lambda b: (b, 0)),
                  pl.BlockSpec((L, MIX), lambda b: (b, 0)),
                  cache_spec, cache_spec,
                  pl.BlockSpec((L, LANE), lambda b: (b, 0)),
                  pl.BlockSpec((1, LANE, L), lambda b: (b, 0, 0)),
                  pl.BlockSpec((1, LANE), lambda b: (0, 0)),
                  pl.BlockSpec((LANE, 1), lambda b: (0, 0)),
                  pl.BlockSpec((1, N_HEADS, P), lambda b: (b, 0, 0))],
        out_specs=[pl.BlockSpec((L, MIX), lambda b: (b, 0)),
                   pl.BlockSpec((L, LANE), lambda b: (b, 0))],
        out_shape=[jax.ShapeDtypeStruct((Bs * L, MIX), BF16),
                   jax.ShapeDtypeStruct((Bs * L, LANE), F32)],
        compiler_params=_params("parallel"),
        name="fox_sample",
    )(q, kn, vn, ck, cv, fg, fgt, bf_pad, bft, rsum)


def _gdn_prep_kernel(ab_ref, alog_ref, dtb_ref, gc_ref, beta_ref, gr_ref, *, rb, C):
    a = ab_ref[:, :LANE]
    bt = ab_ref[:, LANE:]
    g = -jnp.exp(alog_ref[...]) * _softplus(a + dtb_ref[...])
    beta_ref[...] = _sigmoid(bt)
    r_i = _iota2((rb, rb), 0)
    c_i = _iota2((rb, rb), 1)
    same_chunk_tril = jnp.logical_and(r_i // C == c_i // C, c_i <= r_i).astype(F32)
    G = _dot_hi(same_chunk_tril, g)
    gc_ref[...] = G
    eye = (_iota2((LANE, LANE), 0) == _iota2((LANE, LANE), 1)).astype(F32)
    gr_ref[0, 0] = _dot_nt_hi(eye, G)


def gdn_prep(ab, alog_pad, dtb_pad, B, T, rb, C):
    nb = T // rb
    return pl.pallas_call(
        functools.partial(_gdn_prep_kernel, rb=rb, C=C),
        grid=(B, nb),
        in_specs=[pl.BlockSpec((rb, 2 * LANE), lambda b, i: (b * nb + i, 0)),
                  pl.BlockSpec((1, LANE), lambda b, i: (0, 0)),
                  pl.BlockSpec((1, LANE), lambda b, i: (0, 0))],
        out_specs=[pl.BlockSpec((rb, LANE), lambda b, i: (b * nb + i, 0)),
                   pl.BlockSpec((rb, LANE), lambda b, i: (b * nb + i, 0)),
                   pl.BlockSpec((1, 1, LANE, rb), lambda b, i: (b, i, 0, 0))],
        out_shape=[jax.ShapeDtypeStruct((B * T, LANE), F32),
                   jax.ShapeDtypeStruct((B * T, LANE), F32),
                   jax.ShapeDtypeStruct((B, nb, LANE, rb), F32)],
        compiler_params=_params("parallel", "parallel"),
        name="gdn_prep",
    )(ab, alog_pad, dtb_pad)


def _gdn_kernel(xq_ref, xk_ref, xv_ref, wq_ref, wk_ref, wv_ref, gc_ref, beta_ref, gr_ref, s0_ref,
                bq_ref, bk_ref, bv_ref, gain_ref,
                o_ref, sout_ref, cq_ref, ck_ref, cv_ref, S_ref, tail_ref, *, rb, C, nblk, hb):
    hblk = pl.program_id(1)
    i = pl.program_id(2)
    nc = rb // C
    W = hb * HEAD_DIM
    n_iter = C.bit_length() - 2

    @pl.when(i == 0)
    def _():
        S_ref[...] = s0_ref[...]
        tail_ref[...] = jnp.zeros_like(tail_ref)
        tail_ref[0, 5:8, :] = bq_ref[...]
        tail_ref[1, 5:8, :] = bk_ref[...]
        tail_ref[2, 5:8, :] = bv_ref[...]

    row8 = _iota2((8, W), 0)

    def conv_silu(x_ref, w_ref, idx):
        x = x_ref[...]
        w = w_ref[...]
        t8 = tail_ref[idx]
        y = x * w[3:4, :]
        for j in range(1, 4):
            xr = pltpu.roll(x, j, 0)
            head = jnp.where(row8 < j, pltpu.roll(t8, j, 0), xr[:8])
            xs = jnp.concatenate([head, xr[8:]], axis=0)
            y = y + xs * w[3 - j:4 - j, :]
        tail_ref[idx] = x[rb - 8:, :]
        return y * _sigmoid(y)

    qc_all = conv_silu(xq_ref, wq_ref, 0)
    kc_all = conv_silu(xk_ref, wk_ref, 1)
    vv_all = conv_silu(xv_ref, wv_ref, 2)

    r_i = _iota2((C, C), 0)
    c_i = _iota2((C, C), 1)
    incl = c_i <= r_i
    strict = c_i < r_i
    lane = _iota2((rb, LANE), 1)
    gc_blk = gc_ref[...]
    beta_blk = beta_ref[...]

    heads = []
    for hh in range(hb):
        hsl = slice(hh * HEAD_DIM, (hh + 1) * HEAD_DIM)
        qc = qc_all[:, hsl]
        kc = kc_all[:, hsl]
        qn = qc * lax.rsqrt(jnp.sum(qc * qc, axis=-1, keepdims=True) + EPS) * (HEAD_DIM ** -0.5)
        kn = kc * lax.rsqrt(jnp.sum(kc * kc, axis=-1, keepdims=True) + EPS)
        head_id = hblk * hb + hh
        sel = lane == head_id
        g_col = jnp.sum(jnp.where(sel, gc_blk, 0.0), axis=1, keepdims=True)
        b_col = jnp.sum(jnp.where(sel, beta_blk, 0.0), axis=1, keepdims=True)
        g_row = gr_ref[0, 0, pl.ds(head_id, 1), :]
        heads.append(dict(qn=qn, kn=kn, vv=vv_all[:, hsl], g_col=g_col, b_col=b_col, g_row=g_row,
                          eg_col=jnp.exp(g_col), S=S_ref[hh], outs=[]))

    units = []
    for c in range(nc):
        sl = slice(c * C, (c + 1) * C)
        for hd in heads:
            gi = hd["g_col"][sl]
            u_ = dict(hd=hd, gi=gi, bi=hd["b_col"][sl], egi=hd["eg_col"][sl],
                      k=hd["kn"][sl], q=hd["qn"][sl], v=hd["vv"][sl])
            u_["decay"] = jnp.exp(jnp.where(incl, gi - hd["g_row"][:, sl], -jnp.inf))
            units.append(u_)
    for u_ in units:
        u_["kq"] = _mm1(jnp.concatenate([u_["k"], u_["q"]], axis=0), u_["k"], NT_DIMS)
    for u_ in units:
        u_["P"] = -jnp.where(strict, u_["bi"] * u_["kq"][:C] * u_["decay"], 0.0)
        u_["qk"] = jnp.where(incl, u_["kq"][C:] * u_["decay"], 0.0)
        u_["X"] = jnp.concatenate([u_["v"] * u_["bi"], u_["k"] * (u_["bi"] * u_["egi"])], axis=1)
    for _ in range(n_iter):
        for u_ in units:
            x_hi, x_lo = _split_bf16(u_["X"])
            p16 = u_["P"].astype(BF16)
            u_["Y"] = jnp.dot(p16, jnp.concatenate([x_hi, x_lo, p16], axis=1), preferred_element_type=F32)
        for u_ in units:
            u_["X"] = u_["X"] + (u_["Y"][:, :2 * HEAD_DIM] + u_["Y"][:, 2 * HEAD_DIM:4 * HEAD_DIM])
            u_["P"] = u_["Y"][:, 4 * HEAD_DIM:]
    for u_ in units:
        u_["Y"] = _mm2r(u_["P"], u_["X"])
    for u_ in units:
        X = u_["X"] + u_["Y"]
        g_last = u_["gi"][C - 1:C, :]
        u_["u"] = X[:, :HEAD_DIM]
        u_["wq"] = jnp.concatenate([X[:, HEAD_DIM:], u_["q"] * u_["egi"]], axis=0)
        u_["qk_kt"] = jnp.concatenate([u_["qk"], (u_["k"] * jnp.exp(g_last - u_["gi"])).T], axis=0)
        u_["g_last"] = jnp.exp(g_last)

    for c in range(nc):
        cu = units[c * hb:(c + 1) * hb]
        for u_ in cu:
            u_["ws_qs"] = _mm1(u_["wq"], u_["hd"]["S"])
        for u_ in cu:
            u_["v_new"] = u_["u"] - u_["ws_qs"][:C]
            u_["od"] = _mm2r(u_["qk_kt"], u_["v_new"])
        for u_ in cu:
            hd = u_["hd"]
            hd["outs"].append(u_["ws_qs"][C:] + u_["od"][:C])
            hd["S"] = hd["S"] * u_["g_last"] + u_["od"][C:]

    gain = gain_ref[...]
    for hh, hd in enumerate(heads):
        hsl = slice(hh * HEAD_DIM, (hh + 1) * HEAD_DIM)
        S_ref[hh] = hd["S"]
        o = hd["outs"][0] if nc == 1 else jnp.concatenate(hd["outs"], axis=0)
        y = o * lax.rsqrt(jnp.mean(o * o, axis=-1, keepdims=True) + EPS) * gain
        o_ref[:, hsl] = y.astype(BF16)

    @pl.when(i == nblk - 1)
    def _():
        for hh, hd in enumerate(heads):
            sout_ref[0, hh] = hd["S"]
        cq_ref[0] = xq_ref[pl.ds(rb - 3, 3), :]
        ck_ref[0] = xk_ref[pl.ds(rb - 3, 3), :]
        cv_ref[0] = xv_ref[pl.ds(rb - 3, 3), :]


def gdn_mixer(qkv, conv_w, gc, beta, gr, S0, buf0, layer, out_gain, B, T, rb, C, hb):
    nblk = T // rb
    H = N_HEADS
    W = hb * HEAD_DIM
    nhb = H // hb
    x_spec = lambda off: pl.BlockSpec((rb, W), lambda b, h, i: (b * nblk + i, off + h))
    w_spec = lambda off: pl.BlockSpec((4, W), lambda b, h, i: (0, off + h))
    b_spec = lambda off: pl.BlockSpec((None, None, 3, W), lambda b, h, i: (layer, b, 0, off + h))
    col_spec = pl.BlockSpec((rb, LANE), lambda b, h, i: (b * nblk + i, 0))
    s_spec = pl.BlockSpec((1, hb, HEAD_DIM, HEAD_DIM), lambda b, h, i: (b, h, 0, 0))
    c_out = pl.BlockSpec((1, 3, W), lambda b, h, i: (b, 0, h))
    mix, S, cq, ck, cv = pl.pallas_call(
        functools.partial(_gdn_kernel, rb=rb, C=C, nblk=nblk, hb=hb),
        grid=(B, nhb, nblk),
        in_specs=[x_spec(0), x_spec(nhb), x_spec(2 * nhb), w_spec(0), w_spec(nhb), w_spec(2 * nhb),
                  col_spec, col_spec,
                  pl.BlockSpec((1, 1, LANE, rb), lambda b, h, i: (b, i, 0, 0)),
                  pl.BlockSpec((None, None, hb, HEAD_DIM, HEAD_DIM), lambda b, h, i: (layer, b, h, 0, 0)),
                  b_spec(0), b_spec(nhb), b_spec(2 * nhb),
                  pl.BlockSpec((1, HEAD_DIM), lambda b, h, i: (0, 0))],
        out_specs=[pl.BlockSpec((rb, W), lambda b, h, i: (b * nblk + i, h)),
                   s_spec, c_out, c_out, c_out],
        out_shape=[jax.ShapeDtypeStruct((B * T, MIX), BF16),
                   jax.ShapeDtypeStruct((B, H, HEAD_DIM, HEAD_DIM), F32),
                   jax.ShapeDtypeStruct((B, 3, MIX), F32),
                   jax.ShapeDtypeStruct((B, 3, MIX), F32),
                   jax.ShapeDtypeStruct((B, 3, MIX), F32)],
        scratch_shapes=[pltpu.VMEM((hb, HEAD_DIM, HEAD_DIM), F32),
                        pltpu.VMEM((3, 8, W), F32)],
        compiler_params=_params("parallel", "parallel", "arbitrary"),
        name="gdn_mixer",
    )(qkv, qkv, qkv, conv_w, conv_w, conv_w, gc, beta, gr, S0, buf0, buf0, buf0, out_gain.reshape(1, HEAD_DIM))
    return mix, S, jnp.concatenate([cq, ck, cv], axis=-1)


def _pad_vec(v, n):
    return jnp.pad(v.astype(F32), (0, n - v.shape[0])).reshape(1, n)


def kernel(x_prompt, x_sample, mem_prompt, cache_fox_k, cache_fox_v, cache_fox_logf, state_gdn_S, state_gdn_conv, cache_mem_k, cache_mem_v, norm_gain, mem_norm_gain, w_mem_kv, w_in_fox, b_forget, w_in_gdn, gdn_conv_w, gdn_A_log, gdn_dt_bias, gdn_out_norm, w_out, final_norm):
    B, T, D = x_prompt.shape
    Bs, L, _ = x_sample.shape
    depth = norm_gain.shape[0]
    NM = mem_prompt.shape[1]
    P = cache_fox_k.shape[2]
    H = N_HEADS
    scale = HEAD_DIM ** -0.5
    QKV = 3 * MIX

    xp = x_prompt.reshape(B * T, D)
    xs = x_sample.reshape(Bs * L, D)
    mem = mem_prompt.reshape(B * NM, D)

    tq = min(T, 512)
    tk = min(T, 1024)
    n_fox = (depth + 1) // 2
    pk_all = pv_all = sk_all = sv_all = None
    rb_p = min(T, 4 * CHUNK)
    head_major = (0, 1, 3, 2, 4)
    ck_heads = jnp.transpose(cache_fox_k, head_major)
    cv_heads = jnp.transpose(cache_fox_v, head_major)

    p_lf, p_S, p_conv, p_mk, p_mv = [], [], [], [], []
    s_lf, s_S, s_conv = [], [], []

    for i in range(depth):
        j = i // 2
        hm = rms_norm_rows(mem, mem_norm_gain[i], BF16)
        wkv = w_mem_kv[i].astype(BF16)
        mk_p = mm(hm, wkv[:, :MEMW], "f32").reshape(B, NM, MEMW)
        mv_p = mm(hm, wkv[:, MEMW:], "f32").reshape(B, NM, MEMW)
        p_mk.append(mk_p)
        p_mv.append(mv_p)

        if i == 0:
            hp = rms_norm_rows(xp, norm_gain[0], BF16)
            hs = rms_norm_rows(xs, norm_gain[0], BF16)
        w_o = w_out[i].astype(BF16)

        if i % 2 == 0:
            wq, wk, wv, wf, wqm, wz = cast_cols(
                w_in_fox, j, [(0, MIX), (MIX, MIX), (2 * MIX, MIX), (QKV, H), (QKV + H, MEMW),
                              (QKV + H + MEMW, MIX + MEMW)])
            bf_pad = _pad_vec(b_forget[j], LANE)

            q = mm(hp, wq, "scale_bf16", scale * LOG2E)
            pk_all, k16 = proj_kv(hp, wk, pk_all, j, n_fox, B, T)
            pv_all, v16 = proj_kv(hp, wv, pv_all, j, n_fox, B, T)
            fg = mm(hp, wf, "f32")
            qm_p = mm(hp, wqm, "scale_bf16", scale)
            g_p = mm(hp, wz, "silu_bf16")
            logf, qa, ka = fox_prep(fg, bf_pad, B, T, min(T, 512))
            mix_p = fox_attn(q, qa, k16, ka, v16, B, T, tq, tk)
            p_lf.append(logf[:, :H].reshape(B, T, H))

            q = mm(hs, wq, "scale_bf16", scale)
            sk_all, k16 = proj_kv(hs, wk, sk_all, j, n_fox, Bs, L)
            sv_all, v16 = proj_kv(hs, wv, sv_all, j, n_fox, Bs, L)
            fg = mm(hs, wf, "f32")
            qm_s = mm(hs, wqm, "scale_bf16", scale)
            g_s = mm(hs, wz, "silu_bf16")
            fgt = fg.reshape(Bs, L, LANE).transpose(0, 2, 1)
            clf_t = cache_fox_logf[j].astype(F32).transpose(0, 2, 1).reshape(Bs * H, P)
            rsum = suffix_sum(clf_t).reshape(Bs, H, P)
            mix_s, logf_s = fox_sample(q, k16, v16, ck_heads, cv_heads, j,
                                       fg, fgt, bf_pad, bf_pad.reshape(LANE, 1), rsum, Bs, L)
            s_lf.append(logf_s[:, :H].reshape(Bs, L, H))
        else:
            wqkv, wa, wb, wqm, wz = cast_cols(
                w_in_gdn, j, [(0, QKV), (QKV, H), (QKV + H, H), (QKV + 2 * H, MEMW),
                              (QKV + 2 * H + MEMW, MIX + MEMW)])
            wab = jnp.concatenate([wa, wb], axis=1)
            alog_pad = _pad_vec(gdn_A_log[j], LANE)
            dtb_pad = _pad_vec(gdn_dt_bias[j], LANE)

            qkv = mm(hp, wqkv, "f32")
            ab = mm(hp, wab, "f32")
            qm_p = mm(hp, wqm, "scale_bf16", scale)
            g_p = mm(hp, wz, "silu_bf16")
            gc, beta, gr = gdn_prep(ab, alog_pad, dtb_pad, B, T, rb_p, min(CHUNK, T))
            mix_p, S, conv = gdn_mixer(qkv, gdn_conv_w[j], gc, beta, gr,
                                       jnp.zeros((1, B, H, HEAD_DIM, HEAD_DIM), F32),
                                       jnp.zeros((1, B, 3, QKV), F32), 0,
                                       gdn_out_norm[j], B, T, rb_p, min(CHUNK, T), 2)
            p_S.append(S)
            p_conv.append(conv)

            qkv = mm(hs, wqkv, "f32")
            ab = mm(hs, wab, "f32")
            qm_s = mm(hs, wqm, "scale_bf16", scale)
            g_s = mm(hs, wz, "silu_bf16")
            gc, beta, gr = gdn_prep(ab, alog_pad, dtb_pad, Bs, L, L, L)
            mix_s, S, conv = gdn_mixer(qkv, gdn_conv_w[j], gc, beta, gr, state_gdn_S, state_gdn_conv, j,
                                       gdn_out_norm[j], Bs, L, L, L, H)
            s_S.append(S)
            s_conv.append(conv)

        m_p = mem_attn(qm_p, mk_p, mv_p, T)
        m_s = mem_attn(qm_s, cache_mem_k, cache_mem_v, L, layer=i)
        if i + 1 < depth:
            xp, hp = out_proj(mix_p, m_p, g_p, w_o, xp, norm_gain[i + 1], False)
            xs, hs = out_proj(mix_s, m_s, g_s, w_o, xs, norm_gain[i + 1], False)
        else:
            y_prompt = out_proj(mix_p, m_p, g_p, w_o, xp, final_norm, True).reshape(B, T, D)
            y_sample = out_proj(mix_s, m_s, g_s, w_o, xs, final_norm, True).reshape(Bs, L, D)

    mem_shape = (depth, B, NM, N_MEM_HEADS, HEAD_DIM)
    return (y_prompt, y_sample,
            jnp.transpose(pk_all, head_major), jnp.transpose(pv_all, head_major), jnp.stack(p_lf), jnp.stack(p_S), jnp.stack(p_conv),
            jnp.stack(p_mk).reshape(mem_shape), jnp.stack(p_mv).reshape(mem_shape),
            jnp.transpose(sk_all, head_major), jnp.transpose(sv_all, head_major), jnp.stack(s_lf), jnp.stack(s_S), jnp.stack(s_conv))
```

```python
import functools
import math

import jax
import jax.numpy as jnp
from jax import lax
from jax.experimental import pallas as pl
from jax.experimental.pallas import tpu as pltpu

F32 = jnp.float32
BF16 = jnp.bfloat16
HI = lax.Precision.HIGHEST

LANE = 128
HEAD_DIM = 128
N_HEADS = 12
MIX = N_HEADS * HEAD_DIM
N_MEM_HEADS = 4
MEMW = N_MEM_HEADS * HEAD_DIM
CHUNK = 64
EPS = 1e-6
LOG2E = math.log2(math.e)
VMEM_LIMIT = 48 * 1024 * 1024

NN_DIMS = (((1,), (0,)), ((), ()))
NT_DIMS = (((1,), (1,)), ((), ()))


def _params(*sem):
    return pltpu.CompilerParams(dimension_semantics=sem, vmem_limit_bytes=VMEM_LIMIT)


def _sigmoid(x):
    return 1.0 / (1.0 + jnp.exp(-x))


def _log_sigmoid(x):
    return jnp.minimum(x, 0.0) - jnp.log1p(jnp.exp(-jnp.abs(x)))


def _softplus(x):
    return jnp.maximum(x, 0.0) + jnp.log1p(jnp.exp(-jnp.abs(x)))


def _dot_hi(a, b):
    return jnp.dot(a, b, precision=HI, preferred_element_type=F32)


def _dot_nt_hi(a, b):
    return lax.dot_general(a, b, NT_DIMS, precision=HI, preferred_element_type=F32)


def _split_bf16(a):
    hi = a.astype(BF16)
    lo = (a - hi.astype(F32)).astype(BF16)
    return hi, lo


def _mm1(a, b, dims=NN_DIMS):
    return lax.dot_general(a.astype(BF16), b.astype(BF16), dims, preferred_element_type=F32)


def _mm2r(a, b):
    b_hi, b_lo = _split_bf16(b)
    aa = a.astype(BF16)
    return (jnp.dot(aa, b_hi, preferred_element_type=F32)
            + jnp.dot(aa, b_lo, preferred_element_type=F32))


def _iota2(shape, dim):
    return lax.broadcasted_iota(jnp.int32, shape, dim)


def _rms_kernel(x_ref, g_ref, o_ref):
    x = x_ref[...]
    y = x * lax.rsqrt(jnp.mean(x * x, axis=-1, keepdims=True) + EPS)
    o_ref[...] = (y * g_ref[...]).astype(o_ref.dtype)


def rms_norm_rows(x, gain, out_dtype):
    R, D = x.shape
    tm = min(R, 512)
    return pl.pallas_call(
        _rms_kernel,
        grid=(R // tm,),
        in_specs=[pl.BlockSpec((tm, D), lambda i: (i, 0)),
                  pl.BlockSpec((1, D), lambda i: (0, 0))],
        out_specs=pl.BlockSpec((tm, D), lambda i: (i, 0)),
        out_shape=jax.ShapeDtypeStruct((R, D), out_dtype),
        compiler_params=_params("parallel"),
        name="rms_norm",
    )(x, gain.reshape(1, D))


def _cast_cols_kernel(w_ref, *o_refs, cols):
    for o_ref, (start, width) in zip(o_refs, cols):
        x = w_ref[:, start:start + width].astype(BF16)
        pad = o_ref.shape[1] - width
        if pad:
            x = jnp.concatenate([x, jnp.zeros((x.shape[0], pad), BF16)], axis=1)
        o_ref[...] = x


def cast_cols(w_all, layer, cols):
    _, K, N = w_all.shape
    tkb = min(K, 256)
    widths = [-(-width // LANE) * LANE for _, width in cols]
    return pl.pallas_call(
        functools.partial(_cast_cols_kernel, cols=tuple(cols)),
        grid=(K // tkb,),
        in_specs=[pl.BlockSpec((None, tkb, N), lambda i: (layer, i, 0))],
        out_specs=[pl.BlockSpec((tkb, wd), lambda i: (i, 0)) for wd in widths],
        out_shape=[jax.ShapeDtypeStruct((K, wd), BF16) for wd in widths],
        compiler_params=_params("parallel"),
        name="cast_cols",
    )(w_all)


def _mm_kernel(h_ref, w_ref, *o_refs, kind, scale):
    acc = jnp.dot(h_ref[...], w_ref[...], preferred_element_type=F32)
    if kind == "f32":
        o_refs[0][...] = acc
    elif kind == "scale_bf16":
        o_refs[0][...] = (acc * scale).astype(BF16)
    elif kind == "silu_bf16":
        o_refs[0][...] = (acc * _sigmoid(acc)).astype(BF16)
    else:
        raise ValueError(kind)


def mm(h, w, kind, scale=1.0):
    R, K = h.shape
    N = w.shape[1]
    tm = min(R, 1024)
    tn = min(N, 512)
    out_dtypes = {"f32": [F32], "scale_bf16": [BF16], "silu_bf16": [BF16]}[kind]
    outs = pl.pallas_call(
        functools.partial(_mm_kernel, kind=kind, scale=scale),
        grid=(R // tm, N // tn),
        in_specs=[pl.BlockSpec((tm, K), lambda i, n: (i, 0)),
                  pl.BlockSpec((K, tn), lambda i, n: (0, n))],
        out_specs=[pl.BlockSpec((tm, tn), lambda i, n: (i, n)) for _ in out_dtypes],
        out_shape=[jax.ShapeDtypeStruct((R, N), dt) for dt in out_dtypes],
        compiler_params=_params("parallel", "arbitrary"),
        name="proj_" + kind,
    )(h, w)
    return outs if len(outs) > 1 else outs[0]


def _proj_kv_kernel(h_ref, w_ref, stacked_ref, o32_ref, o16_ref):
    del stacked_ref
    nb, _, tt, _ = o32_ref.shape
    acc = jnp.dot(h_ref[...], w_ref[...], preferred_element_type=F32)
    o16_ref[...] = acc.astype(BF16)
    o32_ref[:, 0] = acc[:, :HEAD_DIM].reshape(nb, tt, HEAD_DIM)
    o32_ref[:, 1] = acc[:, HEAD_DIM:].reshape(nb, tt, HEAD_DIM)


def proj_kv(h, w, stacked, layer, n_layers, NB, T):
    R, K = h.shape
    tm = min(R, 1024)
    tt = min(tm, T)
    nb = tm // tt
    nt = T // tt
    return pl.pallas_call(
        _proj_kv_kernel,
        grid=(R // tm, N_HEADS // 2),
        in_specs=[pl.BlockSpec((tm, K), lambda i, hp: (i, 0)),
                  pl.BlockSpec((K, 2 * HEAD_DIM), lambda i, hp: (0, hp)),
                  pl.BlockSpec(memory_space=pl.ANY)],
        out_specs=[pl.BlockSpec((None, nb, 2, tt, HEAD_DIM), lambda i, hp: (layer, i // nt, hp, i % nt, 0)),
                   pl.BlockSpec((tm, 2 * HEAD_DIM), lambda i, hp: (i, hp))],
        out_shape=[jax.ShapeDtypeStruct((n_layers, NB, N_HEADS, T, HEAD_DIM), F32),
                   jax.ShapeDtypeStruct((R, MIX), BF16)],
        input_output_aliases={2: 0},
        compiler_params=_params("parallel", "arbitrary"),
        name="proj_kv",
    )(h, w, stacked)


def _out_proj_kernel(mix_ref, m_ref, g_ref, w_ref, x_ref, gain_ref, *o_refs):
    g = g_ref[...].astype(F32)
    br = jnp.concatenate([(mix_ref[...].astype(F32) * g[:, :MIX]).astype(BF16),
                          (m_ref[...].astype(F32) * g[:, MIX:]).astype(BF16)], axis=1)
    x = x_ref[...] + jnp.dot(br, w_ref[...], preferred_element_type=F32)
    y = x * lax.rsqrt(jnp.mean(x * x, axis=-1, keepdims=True) + EPS) * gain_ref[...]
    o_refs[-1][...] = y.astype(o_refs[-1].dtype)
    if len(o_refs) == 2:
        o_refs[0][...] = x


def out_proj(mix, m, g, w, x, gain, last):
    R, D = x.shape
    BW = MIX + MEMW
    tm = min(R, 256)
    row = lambda width: pl.BlockSpec((tm, width), lambda i: (i, 0))
    if last:
        out_specs, out_shape = [row(D)], [jax.ShapeDtypeStruct((R, D), F32)]
    else:
        out_specs = [row(D), row(D)]
        out_shape = [jax.ShapeDtypeStruct((R, D), F32), jax.ShapeDtypeStruct((R, D), BF16)]
    outs = pl.pallas_call(
        _out_proj_kernel,
        grid=(R // tm,),
        in_specs=[row(MIX), row(MEMW), row(BW),
                  pl.BlockSpec((BW, D), lambda i: (0, 0)),
                  row(D),
                  pl.BlockSpec((1, D), lambda i: (0, 0))],
        out_specs=out_specs,
        out_shape=out_shape,
        compiler_params=_params("parallel"),
        name="out_proj",
    )(mix, m, g, w, x, gain.reshape(1, D))
    return outs[0] if last else outs


def _mem_attn_kernel(q_ref, k_ref, v_ref, o_ref, *, per_head_kv):
    for h in range(N_MEM_HEADS):
        sl = slice(h * HEAD_DIM, (h + 1) * HEAD_DIM)
        q = q_ref[:, sl]
        if per_head_kv:
            k = k_ref[:, h, :].astype(BF16)
            v = v_ref[:, h, :].astype(BF16)
        else:
            k = k_ref[:, sl].astype(BF16)
            v = v_ref[:, sl].astype(BF16)
        s = lax.dot_general(q, k, NT_DIMS, preferred_element_type=F32)
        m = jnp.max(s, axis=1, keepdims=True)
        p = jnp.exp(s - m)
        l = jnp.sum(p, axis=1, keepdims=True)
        o = jnp.dot(p.astype(BF16), v, preferred_element_type=F32) / l
        o_ref[:, sl] = o.astype(BF16)


def mem_attn(q, mk, mv, rows_per_batch, layer=None):
    R = q.shape[0]
    per_head_kv = layer is not None
    tr = min(rows_per_batch, 1024)
    nrb = rows_per_batch // tr
    if per_head_kv:
        NB, NM = mk.shape[1:3]
        kv_spec = pl.BlockSpec((None, None, NM, N_MEM_HEADS, HEAD_DIM), lambda b, i: (layer, b, 0, 0, 0))
    else:
        NB, NM = mk.shape[:2]
        kv_spec = pl.BlockSpec((None, NM, MEMW), lambda b, i: (b, 0, 0))
    return pl.pallas_call(
        functools.partial(_mem_attn_kernel, per_head_kv=per_head_kv),
        grid=(NB, nrb),
        in_specs=[pl.BlockSpec((tr, MEMW), lambda b, i: (b * nrb + i, 0)), kv_spec, kv_spec],
        out_specs=pl.BlockSpec((tr, MEMW), lambda b, i: (b * nrb + i, 0)),
        out_shape=jax.ShapeDtypeStruct((R, MEMW), BF16),
        compiler_params=_params("parallel", "parallel"),
        name="mem_attn",
    )(q, mk, mv)


def _fox_prep_kernel(fg_ref, bf_ref, logf_ref, qa_ref, ka_ref, carry_ref, *, tp):
    @pl.when(pl.program_id(1) == 0)
    def _():
        carry_ref[...] = jnp.zeros_like(carry_ref)

    logf = _log_sigmoid(fg_ref[...] + bf_ref[...])
    tril = (_iota2((tp, tp), 1) <= _iota2((tp, tp), 0)).astype(F32)
    F = _dot_hi(tril, logf) + carry_ref[...]
    carry_ref[...] = F[tp - 1:tp, :]
    logf_ref[...] = logf

    F2 = F * LOG2E
    hi = F2.astype(BF16)
    r1 = F2 - hi.astype(F32)
    mid = r1.astype(BF16)
    lo = (r1 - mid.astype(F32)).astype(BF16)
    col = _iota2((LANE, MIX), 1)
    own_head = _iota2((LANE, MIX), 0) == (col >> 7)
    slot = col & (LANE - 1)

    def spread(x, c):
        sel = jnp.logical_and(own_head, slot == c).astype(BF16)
        return jnp.dot(x, sel, preferred_element_type=F32)

    slot_row = _iota2((1, MIX), 1) & (LANE - 1)
    ones_q = jnp.logical_and(slot_row >= 3, slot_row < 6).astype(F32)
    ones_k = (slot_row < 3).astype(F32)
    qa_ref[...] = (spread(hi, 0) + spread(mid, 1) + spread(lo, 2) + ones_q).astype(BF16)
    ka_ref[...] = (ones_k - (spread(hi, 3) + spread(mid, 4) + spread(lo, 5))).astype(BF16)


def fox_prep(fg, bf_pad, B, T, tp):
    nb = T // tp
    return pl.pallas_call(
        functools.partial(_fox_prep_kernel, tp=tp),
        grid=(B, nb),
        in_specs=[pl.BlockSpec((tp, LANE), lambda b, i: (b * nb + i, 0)),
                  pl.BlockSpec((1, LANE), lambda b, i: (0, 0))],
        out_specs=[pl.BlockSpec((tp, LANE), lambda b, i: (b * nb + i, 0)),
                   pl.BlockSpec((tp, MIX), lambda b, i: (b * nb + i, 0)),
                   pl.BlockSpec((tp, MIX), lambda b, i: (b * nb + i, 0))],
        out_shape=[jax.ShapeDtypeStruct((B * T, LANE), F32),
                   jax.ShapeDtypeStruct((B * T, MIX), BF16),
                   jax.ShapeDtypeStruct((B * T, MIX), BF16)],
        scratch_shapes=[pltpu.VMEM((1, LANE), F32)],
        compiler_params=_params("parallel", "arbitrary"),
        name="fox_prep",
    )(fg, bf_pad)


def _fox_attn_kernel(q_ref, qa_ref, k_ref, ka_ref, v_ref, o_ref, *, tq, tk, n_sub):
    qi = pl.program_id(2)
    sub = tq // n_sub
    qc = jnp.concatenate([q_ref[...], qa_ref[...]], axis=1)
    col = _iota2((sub, tk), 1)
    rows = [qi * tq + s * sub + _iota2((sub, tk), 0) for s in range(n_sub)]

    def body(kj, stats, masked):
        off = pl.multiple_of(kj * tk, tk)
        kc = jnp.concatenate([k_ref[pl.ds(off, tk), :], ka_ref[pl.ds(off, tk), :]], axis=1)
        vb = v_ref[pl.ds(off, tk), :]
        sc_all = lax.dot_general(qc, kc, NT_DIMS, preferred_element_type=F32)
        new = []
        for s in range(n_sub):
            m, l, acc = stats[s]
            sc = sc_all[s * sub:(s + 1) * sub, :]
            if masked:
                sc = jnp.where(col + kj * tk <= rows[s], sc, -jnp.inf)
            m_new = jnp.maximum(m, jnp.max(sc, axis=1, keepdims=True))
            alpha = jnp.exp2(m - m_new)
            p = jnp.exp2(sc - m_new)
            l = alpha * l + jnp.sum(p, axis=1, keepdims=True)
            acc = alpha * acc + jnp.dot(p.astype(BF16), vb, preferred_element_type=F32)
            new.append((m_new, l, acc))
        return tuple(new)

    n_full = (qi * tq + 1) // tk
    n_all = ((qi + 1) * tq + tk - 1) // tk
    stats = tuple((jnp.full((sub, 1), -jnp.inf, F32), jnp.zeros((sub, 1), F32), jnp.zeros((sub, HEAD_DIM), F32))
                  for _ in range(n_sub))
    stats = lax.fori_loop(0, n_full, functools.partial(body, masked=False), stats)
    stats = lax.fori_loop(n_full, n_all, functools.partial(body, masked=True), stats)
    for s in range(n_sub):
        m, l, acc = stats[s]
        o_ref[s * sub:(s + 1) * sub, :] = (acc / l).astype(BF16)


def fox_attn(q, qa, k, ka, v, B, T, tq, tk):
    nq = T // tq
    n_sub = 2 if tq >= 512 else 1
    q_spec = pl.BlockSpec((tq, HEAD_DIM), lambda b, h, i: (b * nq + i, h))
    kv_spec = pl.BlockSpec((T, HEAD_DIM), lambda b, h, i: (b, h))
    return pl.pallas_call(
        functools.partial(_fox_attn_kernel, tq=tq, tk=tk, n_sub=n_sub),
        grid=(B, N_HEADS, nq),
        in_specs=[q_spec, q_spec, kv_spec, kv_spec, kv_spec],
        out_specs=pl.BlockSpec((tq, HEAD_DIM), lambda b, h, i: (b * nq + i, h)),
        out_shape=jax.ShapeDtypeStruct((B * T, MIX), BF16),
        compiler_params=_params("parallel", "parallel", "parallel"),
        name="fox_attn",
    )(q, qa, k, ka, v)


def _suffix_sum_kernel(x_ref, o_ref):
    P = x_ref.shape[1]
    after = (_iota2((P, P), 0) > _iota2((P, P), 1)).astype(F32)
    o_ref[...] = _dot_hi(x_ref[...], after)


def suffix_sum(x):
    return pl.pallas_call(
        _suffix_sum_kernel,
        out_shape=jax.ShapeDtypeStruct(x.shape, F32),
        compiler_params=pltpu.CompilerParams(vmem_limit_bytes=VMEM_LIMIT),
        name="fox_suffix_sum",
    )(x)


def _fox_sample_kernel(q_ref, kn_ref, vn_ref, ck_ref, cv_ref, fg_ref, fgt_ref, bf_ref, bft_ref, r_ref,
                       o_ref, logf_ref, *, L):
    logf = _log_sigmoid(fg_ref[...] + bf_ref[...])
    logf_ref[...] = logf
    r_i = _iota2((L, L), 0)
    c_i = _iota2((L, L), 1)
    causal = c_i <= r_i
    ln_col = _dot_hi(causal.astype(F32), logf)
    logf_t = _log_sigmoid(fgt_ref[0] + bft_ref[...])
    ln_row = _dot_hi(logf_t, (r_i <= c_i).astype(F32))
    for h in range(N_HEADS):
        sl = slice(h * HEAD_DIM, (h + 1) * HEAD_DIM)
        q = q_ref[:, sl]
        kc = ck_ref[h].astype(BF16)
        vc = cv_ref[h].astype(BF16)
        fq = ln_col[:, h:h + 1]
        sc = lax.dot_general(q, kc, NT_DIMS, preferred_element_type=F32) + (fq + r_ref[0, h:h + 1, :])
        sn = lax.dot_general(q, kn_ref[:, sl], NT_DIMS, preferred_element_type=F32) + (fq - ln_row[h:h + 1, :])
        sn = jnp.where(causal, sn, -jnp.inf)
        m = jnp.maximum(jnp.max(sc, axis=1, keepdims=True), jnp.max(sn, axis=1, keepdims=True))
        pc = jnp.exp(sc - m)
        pn = jnp.exp(sn - m)
        l = jnp.sum(pc, axis=1, keepdims=True) + jnp.sum(pn, axis=1, keepdims=True)
        o = (jnp.dot(pc.astype(BF16), vc, preferred_element_type=F32)
             + jnp.dot(pn.astype(BF16), vn_ref[:, sl], preferred_element_type=F32)) / l
        o_ref[:, sl] = o.astype(BF16)


def fox_sample(q, kn, vn, ck, cv, layer, fg, fgt, bf_pad, bft, rsum, Bs, L):
    P = ck.shape[3]
    cache_spec = pl.BlockSpec((None, None, N_HEADS, P, HEAD_DIM), lambda b: (layer, b, 0, 0, 0))
    return pl.pallas_call(
        functools.partial(_fox_sample_kernel, L=L),
        grid=(Bs,),
        in_specs=[pl.BlockSpec((L, MIX), lambda b: (b, 0)),
                  pl.BlockSpec((L, MIX), lambda b: (b, 0)),
                  pl.BlockSpec((L, MIX), lambda b: (b, 0)),
                  cache_spec, cache_spec,
                  pl.BlockSpec((L, LANE), lambda b: (b, 0)),
                  pl.BlockSpec((1, LANE, L), lambda b: (b, 0, 0)),
                  pl.BlockSpec((1, LANE), lambda b: (0, 0)),
                  pl.BlockSpec((LANE, 1), lambda b: (0, 0)),
                  pl.BlockSpec((1, N_HEADS, P), lambda b: (b, 0, 0))],
        out_specs=[pl.BlockSpec((L, MIX), lambda b: (b, 0)),
                   pl.BlockSpec((L, LANE), lambda b: (b, 0))],
        out_shape=[jax.ShapeDtypeStruct((Bs * L, MIX), BF16),
                   jax.ShapeDtypeStruct((Bs * L, LANE), F32)],
        compiler_params=_params("parallel"),
        name="fox_sample",
    )(q, kn, vn, ck, cv, fg, fgt, bf_pad, bft, rsum)


def _gdn_prep_kernel(ab_ref, alog_ref, dtb_ref, gc_ref, beta_ref, gr_ref, *, rb, C):
    a = ab_ref[:, :LANE]
    bt = ab_ref[:, LANE:]
    g = -jnp.exp(alog_ref[...]) * _softplus(a + dtb_ref[...])
    beta_ref[...] = _sigmoid(bt)
    r_i = _iota2((rb, rb), 0)
    c_i = _iota2((rb, rb), 1)
    same_chunk_tril = jnp.logical_and(r_i // C == c_i // C, c_i <= r_i).astype(F32)
    G = _dot_hi(same_chunk_tril, g)
    gc_ref[...] = G
    eye = (_iota2((LANE, LANE), 0) == _iota2((LANE, LANE), 1)).astype(F32)
    gr_ref[0, 0] = _dot_nt_hi(eye, G)


def gdn_prep(ab, alog_pad, dtb_pad, B, T, rb, C):
    nb = T // rb
    return pl.pallas_call(
        functools.partial(_gdn_prep_kernel, rb=rb, C=C),
        grid=(B, nb),
        in_specs=[pl.BlockSpec((rb, 2 * LANE), lambda b, i: (b * nb + i, 0)),
                  pl.BlockSpec((1, LANE), lambda b, i: (0, 0)),
                  pl.BlockSpec((1, LANE), lambda b, i: (0, 0))],
        out_specs=[pl.BlockSpec((rb, LANE), lambda b, i: (b * nb + i, 0)),
                   pl.BlockSpec((rb, LANE), lambda b, i: (b * nb + i, 0)),
                   pl.BlockSpec((1, 1, LANE, rb), lambda b, i: (b, i, 0, 0))],
        out_shape=[jax.ShapeDtypeStruct((B * T, LANE), F32),
                   jax.ShapeDtypeStruct((B * T, LANE), F32),
                   jax.ShapeDtypeStruct((B, nb, LANE, rb), F32)],
        compiler_params=_params("parallel", "parallel"),
        name="gdn_prep",
    )(ab, alog_pad, dtb_pad)


def _gdn_kernel(xq_ref, xk_ref, xv_ref, wq_ref, wk_ref, wv_ref, gc_ref, beta_ref, gr_ref, s0_ref,
                bq_ref, bk_ref, bv_ref, gain_ref,
                o_ref, sout_ref, cq_ref, ck_ref, cv_ref, S_ref, tail_ref, *, rb, C, nblk, hb):
    hblk = pl.program_id(1)
    i = pl.program_id(2)
    nc = rb // C
    W = hb * HEAD_DIM
    n_iter = C.bit_length() - 2

    @pl.when(i == 0)
    def _():
        S_ref[...] = s0_ref[...]
        tail_ref[...] = jnp.zeros_like(tail_ref)
        tail_ref[0, 5:8, :] = bq_ref[...]
        tail_ref[1, 5:8, :] = bk_ref[...]
        tail_ref[2, 5:8, :] = bv_ref[...]

    row8 = _iota2((8, W), 0)

    def conv_silu(x_ref, w_ref, idx):
        x = x_ref[...]
        w = w_ref[...]
        t8 = tail_ref[idx]
        y = x * w[3:4, :]
        for j in range(1, 4):
            xr = pltpu.roll(x, j, 0)
            head = jnp.where(row8 < j, pltpu.roll(t8, j, 0), xr[:8])
            xs = jnp.concatenate([head, xr[8:]], axis=0)
            y = y + xs * w[3 - j:4 - j, :]
        tail_ref[idx] = x[rb - 8:, :]
        return y * _sigmoid(y)

    qc_all = conv_silu(xq_ref, wq_ref, 0)
    kc_all = conv_silu(xk_ref, wk_ref, 1)
    vv_all = conv_silu(xv_ref, wv_ref, 2)

    r_i = _iota2((C, C), 0)
    c_i = _iota2((C, C), 1)
    incl = c_i <= r_i
    strict = c_i < r_i
    lane = _iota2((rb, LANE), 1)
    gc_blk = gc_ref[...]
    beta_blk = beta_ref[...]

    heads = []
    for hh in range(hb):
        hsl = slice(hh * HEAD_DIM, (hh + 1) * HEAD_DIM)
        qc = qc_all[:, hsl]
        kc = kc_all[:, hsl]
        qn = qc * lax.rsqrt(jnp.sum(qc * qc, axis=-1, keepdims=True) + EPS) * (HEAD_DIM ** -0.5)
        kn = kc * lax.rsqrt(jnp.sum(kc * kc, axis=-1, keepdims=True) + EPS)
        head_id = hblk * hb + hh
        sel = lane == head_id
        g_col = jnp.sum(jnp.where(sel, gc_blk, 0.0), axis=1, keepdims=True)
        b_col = jnp.sum(jnp.where(sel, beta_blk, 0.0), axis=1, keepdims=True)
        g_row = gr_ref[0, 0, pl.ds(head_id, 1), :]
        heads.append(dict(qn=qn, kn=kn, vv=vv_all[:, hsl], g_col=g_col, b_col=b_col, g_row=g_row,
                          eg_col=jnp.exp(g_col), S=S_ref[hh], outs=[]))

    units = []
    for c in range(nc):
        sl = slice(c * C, (c + 1) * C)
        for hd in heads:
            gi = hd["g_col"][sl]
            u_ = dict(hd=hd, gi=gi, bi=hd["b_col"][sl], egi=hd["eg_col"][sl],
                      k=hd["kn"][sl], q=hd["qn"][sl], v=hd["vv"][sl])
            u_["decay"] = jnp.exp(jnp.where(incl, gi - hd["g_row"][:, sl], -jnp.inf))
            units.append(u_)
    for u_ in units:
        u_["kq"] = _mm1(jnp.concatenate([u_["k"], u_["q"]], axis=0), u_["k"], NT_DIMS)
    for u_ in units:
        u_["P"] = -jnp.where(strict, u_["bi"] * u_["kq"][:C] * u_["decay"], 0.0)
        u_["qk"] = jnp.where(incl, u_["kq"][C:] * u_["decay"], 0.0)
        u_["X"] = jnp.concatenate([u_["v"] * u_["bi"], u_["k"] * (u_["bi"] * u_["egi"])], axis=1)
    for _ in range(n_iter):
        for u_ in units:
            x_hi, x_lo = _split_bf16(u_["X"])
            p16 = u_["P"].astype(BF16)
            u_["Y"] = jnp.dot(p16, jnp.concatenate([x_hi, x_lo, p16], axis=1), preferred_element_type=F32)
        for u_ in units:
            u_["X"] = u_["X"] + (u_["Y"][:, :2 * HEAD_DIM] + u_["Y"][:, 2 * HEAD_DIM:4 * HEAD_DIM])
            u_["P"] = u_["Y"][:, 4 * HEAD_DIM:]
    for u_ in units:
        u_["Y"] = _mm2r(u_["P"], u_["X"])
    for u_ in units:
        X = u_["X"] + u_["Y"]
        g_last = u_["gi"][C - 1:C, :]
        u_["u"] = X[:, :HEAD_DIM]
        u_["wq"] = jnp.concatenate([X[:, HEAD_DIM:], u_["q"] * u_["egi"]], axis=0)
        u_["qk_kt"] = jnp.concatenate([u_["qk"], (u_["k"] * jnp.exp(g_last - u_["gi"])).T], axis=0)
        u_["g_last"] = jnp.exp(g_last)

    for c in range(nc):
        cu = units[c * hb:(c + 1) * hb]
        for u_ in cu:
            u_["ws_qs"] = _mm1(u_["wq"], u_["hd"]["S"])
        for u_ in cu:
            u_["v_new"] = u_["u"] - u_["ws_qs"][:C]
            u_["od"] = _mm2r(u_["qk_kt"], u_["v_new"])
        for u_ in cu:
            hd = u_["hd"]
            hd["outs"].append(u_["ws_qs"][C:] + u_["od"][:C])
            hd["S"] = hd["S"] * u_["g_last"] + u_["od"][C:]

    gain = gain_ref[...]
    for hh, hd in enumerate(heads):
        hsl = slice(hh * HEAD_DIM, (hh + 1) * HEAD_DIM)
        S_ref[hh] = hd["S"]
        o = hd["outs"][0] if nc == 1 else jnp.concatenate(hd["outs"], axis=0)
        y = o * lax.rsqrt(jnp.mean(o * o, axis=-1, keepdims=True) + EPS) * gain
        o_ref[:, hsl] = y.astype(BF16)

    @pl.when(i == nblk - 1)
    def _():
        for hh, hd in enumerate(heads):
            sout_ref[0, hh] = hd["S"]
        cq_ref[0] = xq_ref[pl.ds(rb - 3, 3), :]
        ck_ref[0] = xk_ref[pl.ds(rb - 3, 3), :]
        cv_ref[0] = xv_ref[pl.ds(rb - 3, 3), :]


def gdn_mixer(qkv, conv_w, gc, beta, gr, S0, buf0, layer, out_gain, B, T, rb, C, hb):
    nblk = T // rb
    H = N_HEADS
    W = hb * HEAD_DIM
    nhb = H // hb
    x_spec = lambda off: pl.BlockSpec((rb, W), lambda b, h, i: (b * nblk + i, off + h))
    w_spec = lambda off: pl.BlockSpec((4, W), lambda b, h, i: (0, off + h))
    b_spec = lambda off: pl.BlockSpec((None, None, 3, W), lambda b, h, i: (layer, b, 0, off + h))
    col_spec = pl.BlockSpec((rb, LANE), lambda b, h, i: (b * nblk + i, 0))
    s_spec = pl.BlockSpec((1, hb, HEAD_DIM, HEAD_DIM), lambda b, h, i: (b, h, 0, 0))
    c_out = pl.BlockSpec((1, 3, W), lambda b, h, i: (b, 0, h))
    mix, S, cq, ck, cv = pl.pallas_call(
        functools.partial(_gdn_kernel, rb=rb, C=C, nblk=nblk, hb=hb),
        grid=(B, nhb, nblk),
        in_specs=[x_spec(0), x_spec(nhb), x_spec(2 * nhb), w_spec(0), w_spec(nhb), w_spec(2 * nhb),
                  col_spec, col_spec,
                  pl.BlockSpec((1, 1, LANE, rb), lambda b, h, i: (b, i, 0, 0)),
                  pl.BlockSpec((None, None, hb, HEAD_DIM, HEAD_DIM), lambda b, h, i: (layer, b, h, 0, 0)),
                  b_spec(0), b_spec(nhb), b_spec(2 * nhb),
                  pl.BlockSpec((1, HEAD_DIM), lambda b, h, i: (0, 0))],
        out_specs=[pl.BlockSpec((rb, W), lambda b, h, i: (b * nblk + i, h)),
                   s_spec, c_out, c_out, c_out],
        out_shape=[jax.ShapeDtypeStruct((B * T, MIX), BF16),
                   jax.ShapeDtypeStruct((B, H, HEAD_DIM, HEAD_DIM), F32),
                   jax.ShapeDtypeStruct((B, 3, MIX), F32),
                   jax.ShapeDtypeStruct((B, 3, MIX), F32),
                   jax.ShapeDtypeStruct((B, 3, MIX), F32)],
        scratch_shapes=[pltpu.VMEM((hb, HEAD_DIM, HEAD_DIM), F32),
                        pltpu.VMEM((3, 8, W), F32)],
        compiler_params=_params("parallel", "parallel", "arbitrary"),
        name="gdn_mixer",
    )(qkv, qkv, qkv, conv_w, conv_w, conv_w, gc, beta, gr, S0, buf0, buf0, buf0, out_gain.reshape(1, HEAD_DIM))
    return mix, S, jnp.concatenate([cq, ck, cv], axis=-1)


def _pad_vec(v, n):
    return jnp.pad(v.astype(F32), (0, n - v.shape[0])).reshape(1, n)


def kernel(x_prompt, x_sample, mem_prompt, cache_fox_k, cache_fox_v, cache_fox_logf, state_gdn_S, state_gdn_conv, cache_mem_k, cache_mem_v, norm_gain, mem_norm_gain, w_mem_kv, w_in_fox, b_forget, w_in_gdn, gdn_conv_w, gdn_A_log, gdn_dt_bias, gdn_out_norm, w_out, final_norm):
    B, T, D = x_prompt.shape
    Bs, L, _ = x_sample.shape
    depth = norm_gain.shape[0]
    NM = mem_prompt.shape[1]
    P = cache_fox_k.shape[2]
    H = N_HEADS
    scale = HEAD_DIM ** -0.5
    QKV = 3 * MIX

    xp = x_prompt.reshape(B * T, D)
    xs = x_sample.reshape(Bs * L, D)
    mem = mem_prompt.reshape(B * NM, D)

    tq = min(T, 512)
    tk = min(T, 1024)
    n_fox = (depth + 1) // 2
    pk_all = pv_all = jnp.zeros((n_fox, B, H, T, HEAD_DIM), F32)
    sk_all = sv_all = jnp.zeros((n_fox, Bs, H, L, HEAD_DIM), F32)
    rb_p = min(T, CHUNK)
    head_major = (0, 1, 3, 2, 4)
    ck_heads = jnp.transpose(cache_fox_k, head_major)
    cv_heads = jnp.transpose(cache_fox_v, head_major)

    p_lf, p_S, p_conv, p_mk, p_mv = [], [], [], [], []
    s_lf, s_S, s_conv = [], [], []

    for i in range(depth):
        j = i // 2
        hm = rms_norm_rows(mem, mem_norm_gain[i], BF16)
        wkv = w_mem_kv[i].astype(BF16)
        mk_p = mm(hm, wkv[:, :MEMW], "f32").reshape(B, NM, MEMW)
        mv_p = mm(hm, wkv[:, MEMW:], "f32").reshape(B, NM, MEMW)
        p_mk.append(mk_p)
        p_mv.append(mv_p)

        if i == 0:
            hp = rms_norm_rows(xp, norm_gain[0], BF16)
            hs = rms_norm_rows(xs, norm_gain[0], BF16)
        w_o = w_out[i].astype(BF16)

        if i % 2 == 0:
            wq, wk, wv, wf, wqm, wz = cast_cols(
                w_in_fox, j, [(0, MIX), (MIX, MIX), (2 * MIX, MIX), (QKV, H), (QKV + H, MEMW),
                              (QKV + H + MEMW, MIX + MEMW)])
            bf_pad = _pad_vec(b_forget[j], LANE)

            q = mm(hp, wq, "scale_bf16", scale * LOG2E)
            pk_all, k16 = proj_kv(hp, wk, pk_all, j, n_fox, B, T)
            pv_all, v16 = proj_kv(hp, wv, pv_all, j, n_fox, B, T)
            fg = mm(hp, wf, "f32")
            qm_p = mm(hp, wqm, "scale_bf16", scale)
            g_p = mm(hp, wz, "silu_bf16")
            logf, qa, ka = fox_prep(fg, bf_pad, B, T, min(T, 512))
            mix_p = fox_attn(q, qa, k16, ka, v16, B, T, tq, tk)
            p_lf.append(logf[:, :H].reshape(B, T, H))

            q = mm(hs, wq, "scale_bf16", scale)
            sk_all, k16 = proj_kv(hs, wk, sk_all, j, n_fox, Bs, L)
            sv_all, v16 = proj_kv(hs, wv, sv_all, j, n_fox, Bs, L)
            fg = mm(hs, wf, "f32")
            qm_s = mm(hs, wqm, "scale_bf16", scale)
            g_s = mm(hs, wz, "silu_bf16")
            fgt = fg.reshape(Bs, L, LANE).transpose(0, 2, 1)
            clf_t = cache_fox_logf[j].astype(F32).transpose(0, 2, 1).reshape(Bs * H, P)
            rsum = suffix_sum(clf_t).reshape(Bs, H, P)
            mix_s, logf_s = fox_sample(q, k16, v16, ck_heads, cv_heads, j,
                                       fg, fgt, bf_pad, bf_pad.reshape(LANE, 1), rsum, Bs, L)
            s_lf.append(logf_s[:, :H].reshape(Bs, L, H))
        else:
            wqkv, wa, wb, wqm, wz = cast_cols(
                w_in_gdn, j, [(0, QKV), (QKV, H), (QKV + H, H), (QKV + 2 * H, MEMW),
                              (QKV + 2 * H + MEMW, MIX + MEMW)])
            wab = jnp.concatenate([wa, wb], axis=1)
            alog_pad = _pad_vec(gdn_A_log[j], LANE)
            dtb_pad = _pad_vec(gdn_dt_bias[j], LANE)

            qkv = mm(hp, wqkv, "f32")
            ab = mm(hp, wab, "f32")
            qm_p = mm(hp, wqm, "scale_bf16", scale)
            g_p = mm(hp, wz, "silu_bf16")
            gc, beta, gr = gdn_prep(ab, alog_pad, dtb_pad, B, T, rb_p, min(CHUNK, T))
            mix_p, S, conv = gdn_mixer(qkv, gdn_conv_w[j], gc, beta, gr,
                                       jnp.zeros((1, B, H, HEAD_DIM, HEAD_DIM), F32),
                                       jnp.zeros((1, B, 3, QKV), F32), 0,
                                       gdn_out_norm[j], B, T, rb_p, min(CHUNK, T), H)
            p_S.append(S)
            p_conv.append(conv)

            qkv = mm(hs, wqkv, "f32")
            ab = mm(hs, wab, "f32")
            qm_s = mm(hs, wqm, "scale_bf16", scale)
            g_s = mm(hs, wz, "silu_bf16")
            gc, beta, gr = gdn_prep(ab, alog_pad, dtb_pad, Bs, L, L, L)
            mix_s, S, conv = gdn_mixer(qkv, gdn_conv_w[j], gc, beta, gr, state_gdn_S, state_gdn_conv, j,
                                       gdn_out_norm[j], Bs, L, L, L, H)
            s_S.append(S)
            s_conv.append(conv)

        m_p = mem_attn(qm_p, mk_p, mv_p, T)
        m_s = mem_attn(qm_s, cache_mem_k, cache_mem_v, L, layer=i)
        if i + 1 < depth:
            xp, hp = out_proj(mix_p, m_p, g_p, w_o, xp, norm_gain[i + 1], False)
            xs, hs = out_proj(mix_s, m_s, g_s, w_o, xs, norm_gain[i + 1], False)
        else:
            y_prompt = out_proj(mix_p, m_p, g_p, w_o, xp, final_norm, True).reshape(B, T, D)
            y_sample = out_proj(mix_s, m_s, g_s, w_o, xs, final_norm, True).reshape(Bs, L, D)

    mem_shape = (depth, B, NM, N_MEM_HEADS, HEAD_DIM)
    return (y_prompt, y_sample,
            jnp.transpose(pk_all, head_major), jnp.transpose(pv_all, head_major), jnp.stack(p_lf), jnp.stack(p_S), jnp.stack(p_conv),
            jnp.stack(p_mk).reshape(mem_shape), jnp.stack(p_mv).reshape(mem_shape),
            jnp.transpose(sk_all, head_major), jnp.transpose(sv_all, head_major), jnp.stack(s_lf), jnp.stack(s_S), jnp.stack(s_conv))
```

```python
import functools
import math

import jax
import jax.numpy as jnp
from jax import lax
from jax.experimental import pallas as pl
from jax.experimental.pallas import tpu as pltpu

F32 = jnp.float32
BF16 = jnp.bfloat16
HI = lax.Precision.HIGHEST

LANE = 128
HEAD_DIM = 128
N_HEADS = 12
MIX = N_HEADS * HEAD_DIM
N_MEM_HEADS = 4
MEMW = N_MEM_HEADS * HEAD_DIM
CHUNK = 64
EPS = 1e-6
LOG2E = math.log2(math.e)
VMEM_LIMIT = 48 * 1024 * 1024

NN_DIMS = (((1,), (0,)), ((), ()))
NT_DIMS = (((1,), (1,)), ((), ()))


def _params(*sem):
    return pltpu.CompilerParams(dimension_semantics=sem, vmem_limit_bytes=VMEM_LIMIT)


def _sigmoid(x):
    return 1.0 / (1.0 + jnp.exp(-x))


def _log_sigmoid(x):
    return jnp.minimum(x, 0.0) - jnp.log1p(jnp.exp(-jnp.abs(x)))


def _softplus(x):
    return jnp.maximum(x, 0.0) + jnp.log1p(jnp.exp(-jnp.abs(x)))


def _dot_hi(a, b):
    return jnp.dot(a, b, precision=HI, preferred_element_type=F32)


def _dot_nt_hi(a, b):
    return lax.dot_general(a, b, NT_DIMS, precision=HI, preferred_element_type=F32)


def _split_bf16(a):
    hi = a.astype(BF16)
    lo = (a - hi.astype(F32)).astype(BF16)
    return hi, lo


def _mm1(a, b, dims=NN_DIMS):
    return lax.dot_general(a.astype(BF16), b.astype(BF16), dims, preferred_element_type=F32)


def _mm2r(a, b):
    b_hi, b_lo = _split_bf16(b)
    aa = a.astype(BF16)
    return (jnp.dot(aa, b_hi, preferred_element_type=F32)
            + jnp.dot(aa, b_lo, preferred_element_type=F32))


def _iota2(shape, dim):
    return lax.broadcasted_iota(jnp.int32, shape, dim)


def _rms_kernel(x_ref, g_ref, o_ref):
    x = x_ref[...]
    y = x * lax.rsqrt(jnp.mean(x * x, axis=-1, keepdims=True) + EPS)
    o_ref[...] = (y * g_ref[...]).astype(o_ref.dtype)


def rms_norm_rows(x, gain, out_dtype):
    R, D = x.shape
    tm = min(R, 512)
    return pl.pallas_call(
        _rms_kernel,
        grid=(R // tm,),
        in_specs=[pl.BlockSpec((tm, D), lambda i: (i, 0)),
                  pl.BlockSpec((1, D), lambda i: (0, 0))],
        out_specs=pl.BlockSpec((tm, D), lambda i: (i, 0)),
        out_shape=jax.ShapeDtypeStruct((R, D), out_dtype),
        compiler_params=_params("parallel"),
        name="rms_norm",
    )(x, gain.reshape(1, D))


def _cast_cols_kernel(w_ref, *o_refs, cols):
    for o_ref, (start, width) in zip(o_refs, cols):
        x = w_ref[:, start:start + width].astype(BF16)
        pad = o_ref.shape[1] - width
        if pad:
            x = jnp.concatenate([x, jnp.zeros((x.shape[0], pad), BF16)], axis=1)
        o_ref[...] = x


def cast_cols(w_all, layer, cols):
    _, K, N = w_all.shape
    tkb = min(K, 256)
    widths = [-(-width // LANE) * LANE for _, width in cols]
    return pl.pallas_call(
        functools.partial(_cast_cols_kernel, cols=tuple(cols)),
        grid=(K // tkb,),
        in_specs=[pl.BlockSpec((None, tkb, N), lambda i: (layer, i, 0))],
        out_specs=[pl.BlockSpec((tkb, wd), lambda i: (i, 0)) for wd in widths],
        out_shape=[jax.ShapeDtypeStruct((K, wd), BF16) for wd in widths],
        compiler_params=_params("parallel"),
        name="cast_cols",
    )(w_all)


def _mm_kernel(h_ref, w_ref, *o_refs, kind, scale):
    acc = jnp.dot(h_ref[...], w_ref[...], preferred_element_type=F32)
    if kind == "f32":
        o_refs[0][...] = acc
    elif kind == "scale_bf16":
        o_refs[0][...] = (acc * scale).astype(BF16)
    elif kind == "silu_bf16":
        o_refs[0][...] = (acc * _sigmoid(acc)).astype(BF16)
    else:
        raise ValueError(kind)


def mm(h, w, kind, scale=1.0):
    R, K = h.shape
    N = w.shape[1]
    tm = min(R, 1024)
    tn = min(N, 512)
    out_dtypes = {"f32": [F32], "scale_bf16": [BF16], "silu_bf16": [BF16]}[kind]
    outs = pl.pallas_call(
        functools.partial(_mm_kernel, kind=kind, scale=scale),
        grid=(R // tm, N // tn),
        in_specs=[pl.BlockSpec((tm, K), lambda i, n: (i, 0)),
                  pl.BlockSpec((K, tn), lambda i, n: (0, n))],
        out_specs=[pl.BlockSpec((tm, tn), lambda i, n: (i, n)) for _ in out_dtypes],
        out_shape=[jax.ShapeDtypeStruct((R, N), dt) for dt in out_dtypes],
        compiler_params=_params("parallel", "arbitrary"),
        name="proj_" + kind,
    )(h, w)
    return outs if len(outs) > 1 else outs[0]


def _proj_kv_kernel(h_ref, w_ref, stacked_ref, o32_ref, o16_ref):
    del stacked_ref
    nb, _, tt, _ = o32_ref.shape
    acc = jnp.dot(h_ref[...], w_ref[...], preferred_element_type=F32)
    o16_ref[...] = acc.astype(BF16)
    o32_ref[:, 0] = acc[:, :HEAD_DIM].reshape(nb, tt, HEAD_DIM)
    o32_ref[:, 1] = acc[:, HEAD_DIM:].reshape(nb, tt, HEAD_DIM)


def proj_kv(h, w, stacked, layer, n_layers, NB, T):
    R, K = h.shape
    tm = min(R, 1024)
    tt = min(tm, T)
    nb = tm // tt
    nt = T // tt
    return pl.pallas_call(
        _proj_kv_kernel,
        grid=(R // tm, N_HEADS // 2),
        in_specs=[pl.BlockSpec((tm, K), lambda i, hp: (i, 0)),
                  pl.BlockSpec((K, 2 * HEAD_DIM), lambda i, hp: (0, hp)),
                  pl.BlockSpec(memory_space=pl.ANY)],
        out_specs=[pl.BlockSpec((None, nb, 2, tt, HEAD_DIM), lambda i, hp: (layer, i // nt, hp, i % nt, 0)),
                   pl.BlockSpec((tm, 2 * HEAD_DIM), lambda i, hp: (i, hp))],
        out_shape=[jax.ShapeDtypeStruct((n_layers, NB, N_HEADS, T, HEAD_DIM), F32),
                   jax.ShapeDtypeStruct((R, MIX), BF16)],
        input_output_aliases={2: 0},
        compiler_params=_params("parallel", "arbitrary"),
        name="proj_kv",
    )(h, w, stacked)


def _out_proj_kernel(mix_ref, m_ref, g_ref, w_ref, x_ref, gain_ref, *o_refs):
    g = g_ref[...].astype(F32)
    br = jnp.concatenate([(mix_ref[...].astype(F32) * g[:, :MIX]).astype(BF16),
                          (m_ref[...].astype(F32) * g[:, MIX:]).astype(BF16)], axis=1)
    x = x_ref[...] + jnp.dot(br, w_ref[...], preferred_element_type=F32)
    y = x * lax.rsqrt(jnp.mean(x * x, axis=-1, keepdims=True) + EPS) * gain_ref[...]
    o_refs[-1][...] = y.astype(o_refs[-1].dtype)
    if len(o_refs) == 2:
        o_refs[0][...] = x


def out_proj(mix, m, g, w, x, gain, last):
    R, D = x.shape
    BW = MIX + MEMW
    tm = min(R, 256)
    row = lambda width: pl.BlockSpec((tm, width), lambda i: (i, 0))
    if last:
        out_specs, out_shape = [row(D)], [jax.ShapeDtypeStruct((R, D), F32)]
    else:
        out_specs = [row(D), row(D)]
        out_shape = [jax.ShapeDtypeStruct((R, D), F32), jax.ShapeDtypeStruct((R, D), BF16)]
    outs = pl.pallas_call(
        _out_proj_kernel,
        grid=(R // tm,),
        in_specs=[row(MIX), row(MEMW), row(BW),
                  pl.BlockSpec((BW, D), lambda i: (0, 0)),
                  row(D),
                  pl.BlockSpec((1, D), lambda i: (0, 0))],
        out_specs=out_specs,
        out_shape=out_shape,
        compiler_params=_params("parallel"),
        name="out_proj",
    )(mix, m, g, w, x, gain.reshape(1, D))
    return outs[0] if last else outs


def _mem_attn_kernel(q_ref, k_ref, v_ref, o_ref, *, per_head_kv):
    for h in range(N_MEM_HEADS):
        sl = slice(h * HEAD_DIM, (h + 1) * HEAD_DIM)
        q = q_ref[:, sl]
        if per_head_kv:
            k = k_ref[:, h, :].astype(BF16)
            v = v_ref[:, h, :].astype(BF16)
        else:
            k = k_ref[:, sl].astype(BF16)
            v = v_ref[:, sl].astype(BF16)
        s = lax.dot_general(q, k, NT_DIMS, preferred_element_type=F32)
        m = jnp.max(s, axis=1, keepdims=True)
        p = jnp.exp(s - m)
        l = jnp.sum(p, axis=1, keepdims=True)
        o = jnp.dot(p.astype(BF16), v, preferred_element_type=F32) / l
        o_ref[:, sl] = o.astype(BF16)


def mem_attn(q, mk, mv, rows_per_batch, layer=None):
    R = q.shape[0]
    per_head_kv = layer is not None
    tr = min(rows_per_batch, 1024)
    nrb = rows_per_batch // tr
    if per_head_kv:
        NB, NM = mk.shape[1:3]
        kv_spec = pl.BlockSpec((None, None, NM, N_MEM_HEADS, HEAD_DIM), lambda b, i: (layer, b, 0, 0, 0))
    else:
        NB, NM = mk.shape[:2]
        kv_spec = pl.BlockSpec((None, NM, MEMW), lambda b, i: (b, 0, 0))
    return pl.pallas_call(
        functools.partial(_mem_attn_kernel, per_head_kv=per_head_kv),
        grid=(NB, nrb),
        in_specs=[pl.BlockSpec((tr, MEMW), lambda b, i: (b * nrb + i, 0)), kv_spec, kv_spec],
        out_specs=pl.BlockSpec((tr, MEMW), lambda b, i: (b * nrb + i, 0)),
        out_shape=jax.ShapeDtypeStruct((R, MEMW), BF16),
        compiler_params=_params("parallel", "parallel"),
        name="mem_attn",
    )(q, mk, mv)


def _fox_prep_kernel(fg_ref, bf_ref, logf_ref, qa_ref, ka_ref, carry_ref, *, tp):
    @pl.when(pl.program_id(1) == 0)
    def _():
        carry_ref[...] = jnp.zeros_like(carry_ref)

    logf = _log_sigmoid(fg_ref[...] + bf_ref[...])
    tril = (_iota2((tp, tp), 1) <= _iota2((tp, tp), 0)).astype(F32)
    F = _dot_hi(tril, logf) + carry_ref[...]
    carry_ref[...] = F[tp - 1:tp, :]
    logf_ref[...] = logf

    F2 = F * LOG2E
    hi = F2.astype(BF16)
    r1 = F2 - hi.astype(F32)
    mid = r1.astype(BF16)
    lo = (r1 - mid.astype(F32)).astype(BF16)
    col = _iota2((LANE, MIX), 1)
    own_head = _iota2((LANE, MIX), 0) == (col >> 7)
    slot = col & (LANE - 1)

    def spread(x, c):
        sel = jnp.logical_and(own_head, slot == c).astype(BF16)
        return jnp.dot(x, sel, preferred_element_type=F32)

    slot_row = _iota2((1, MIX), 1) & (LANE - 1)
    ones_q = jnp.logical_and(slot_row >= 3, slot_row < 6).astype(F32)
    ones_k = (slot_row < 3).astype(F32)
    qa_ref[...] = (spread(hi, 0) + spread(mid, 1) + spread(lo, 2) + ones_q).astype(BF16)
    ka_ref[...] = (ones_k - (spread(hi, 3) + spread(mid, 4) + spread(lo, 5))).astype(BF16)


def fox_prep(fg, bf_pad, B, T, tp):
    nb = T // tp
    return pl.pallas_call(
        functools.partial(_fox_prep_kernel, tp=tp),
        grid=(B, nb),
        in_specs=[pl.BlockSpec((tp, LANE), lambda b, i: (b * nb + i, 0)),
                  pl.BlockSpec((1, LANE), lambda b, i: (0, 0))],
        out_specs=[pl.BlockSpec((tp, LANE), lambda b, i: (b * nb + i, 0)),
                   pl.BlockSpec((tp, MIX), lambda b, i: (b * nb + i, 0)),
                   pl.BlockSpec((tp, MIX), lambda b, i: (b * nb + i, 0))],
        out_shape=[jax.ShapeDtypeStruct((B * T, LANE), F32),
                   jax.ShapeDtypeStruct((B * T, MIX), BF16),
                   jax.ShapeDtypeStruct((B * T, MIX), BF16)],
        scratch_shapes=[pltpu.VMEM((1, LANE), F32)],
        compiler_params=_params("parallel", "arbitrary"),
        name="fox_prep",
    )(fg, bf_pad)


def _fox_attn_kernel(q_ref, qa_ref, k_ref, ka_ref, v_ref, o_ref, *, tq, tk, n_sub):
    qi = pl.program_id(2)
    sub = tq // n_sub
    qc = jnp.concatenate([q_ref[...], qa_ref[...]], axis=1)
    col = _iota2((sub, tk), 1)
    rows = [qi * tq + s * sub + _iota2((sub, tk), 0) for s in range(n_sub)]

    def body(kj, stats, masked):
        off = pl.multiple_of(kj * tk, tk)
        kc = jnp.concatenate([k_ref[pl.ds(off, tk), :], ka_ref[pl.ds(off, tk), :]], axis=1)
        vb = v_ref[pl.ds(off, tk), :]
        sc_all = lax.dot_general(qc, kc, NT_DIMS, preferred_element_type=F32)
        new = []
        for s in range(n_sub):
            m, l, acc = stats[s]
            sc = sc_all[s * sub:(s + 1) * sub, :]
            if masked:
                sc = jnp.where(col + kj * tk <= rows[s], sc, -jnp.inf)
            m_new = jnp.maximum(m, jnp.max(sc, axis=1, keepdims=True))
            alpha = jnp.exp2(m - m_new)
            p = jnp.exp2(sc - m_new)
            l = alpha * l + jnp.sum(p, axis=1, keepdims=True)
            acc = alpha * acc + jnp.dot(p.astype(BF16), vb, preferred_element_type=F32)
            new.append((m_new, l, acc))
        return tuple(new)

    n_full = (qi * tq + 1) // tk
    n_all = ((qi + 1) * tq + tk - 1) // tk
    stats = tuple((jnp.full((sub, 1), -jnp.inf, F32), jnp.zeros((sub, 1), F32), jnp.zeros((sub, HEAD_DIM), F32))
                  for _ in range(n_sub))
    stats = lax.fori_loop(0, n_full, functools.partial(body, masked=False), stats)
    stats = lax.fori_loop(n_full, n_all, functools.partial(body, masked=True), stats)
    for s in range(n_sub):
        m, l, acc = stats[s]
        o_ref[s * sub:(s + 1) * sub, :] = (acc / l).astype(BF16)


def fox_attn(q, qa, k, ka, v, B, T, tq, tk):
    nq = T // tq
    n_sub = 2 if tq >= 512 else 1
    q_spec = pl.BlockSpec((tq, HEAD_DIM), lambda b, h, i: (b * nq + i, h))
    kv_spec = pl.BlockSpec((T, HEAD_DIM), lambda b, h, i: (b, h))
    return pl.pallas_call(
        functools.partial(_fox_attn_kernel, tq=tq, tk=tk, n_sub=n_sub),
        grid=(B, N_HEADS, nq),
        in_specs=[q_spec, q_spec, kv_spec, kv_spec, kv_spec],
        out_specs=pl.BlockSpec((tq, HEAD_DIM), lambda b, h, i: (b * nq + i, h)),
        out_shape=jax.ShapeDtypeStruct((B * T, MIX), BF16),
        compiler_params=_params("parallel", "parallel", "parallel"),
        name="fox_attn",
    )(q, qa, k, ka, v)


def _suffix_sum_kernel(x_ref, o_ref):
    P = x_ref.shape[1]
    after = (_iota2((P, P), 0) > _iota2((P, P), 1)).astype(F32)
    o_ref[...] = _dot_hi(x_ref[...], after)


def suffix_sum(x):
    return pl.pallas_call(
        _suffix_sum_kernel,
        out_shape=jax.ShapeDtypeStruct(x.shape, F32),
        compiler_params=pltpu.CompilerParams(vmem_limit_bytes=VMEM_LIMIT),
        name="fox_suffix_sum",
    )(x)


def _fox_sample_kernel(q_ref, kn_ref, vn_ref, ck_ref, cv_ref, fg_ref, fgt_ref, bf_ref, bft_ref, r_ref,
                       o_ref, logf_ref, *, L):
    logf = _log_sigmoid(fg_ref[...] + bf_ref[...])
    logf_ref[...] = logf
    r_i = _iota2((L, L), 0)
    c_i = _iota2((L, L), 1)
    causal = c_i <= r_i
    ln_col = _dot_hi(causal.astype(F32), logf)
    logf_t = _log_sigmoid(fgt_ref[0] + bft_ref[...])
    ln_row = _dot_hi(logf_t, (r_i <= c_i).astype(F32))
    for h in range(N_HEADS):
        sl = slice(h * HEAD_DIM, (h + 1) * HEAD_DIM)
        q = q_ref[:, sl]
        kc = ck_ref[h].astype(BF16)
        vc = cv_ref[h].astype(BF16)
        fq = ln_col[:, h:h + 1]
        sc = lax.dot_general(q, kc, NT_DIMS, preferred_element_type=F32) + (fq + r_ref[0, h:h + 1, :])
        sn = lax.dot_general(q, kn_ref[:, sl], NT_DIMS, preferred_element_type=F32) + (fq - ln_row[h:h + 1, :])
        sn = jnp.where(causal, sn, -jnp.inf)
        m = jnp.maximum(jnp.max(sc, axis=1, keepdims=True), jnp.max(sn, axis=1, keepdims=True))
        pc = jnp.exp(sc - m)
        pn = jnp.exp(sn - m)
        l = jnp.sum(pc, axis=1, keepdims=True) + jnp.sum(pn, axis=1, keepdims=True)
        o = (jnp.dot(pc.astype(BF16), vc, preferred_element_type=F32)
             + jnp.dot(pn.astype(BF16), vn_ref[:, sl], preferred_element_type=F32)) / l
        o_ref[:, sl] = o.astype(BF16)


def fox_sample(q, kn, vn, ck, cv, layer, fg, fgt, bf_pad, bft, rsum, Bs, L):
    P = ck.shape[3]
    cache_spec = pl.BlockSpec((None, None, N_HEADS, P, HEAD_DIM), lambda b: (layer, b, 0, 0, 0))
    return pl.pallas_call(
        functools.partial(_fox_sample_kernel, L=L),
        grid=(Bs,),
        in_specs=[pl.BlockSpec((L, MIX), lambda b: (b, 0)),
                  pl.BlockSpec((L, MIX), lambda b: (b, 0)),
                  pl.BlockSpec((L, MIX), lambda b: (b, 0)),
                  cache_spec, cache_spec,
                  pl.BlockSpec((L, LANE), lambda b: (b, 0)),
                  pl.BlockSpec((1, LANE, L), lambda b: (b, 0, 0)),
                  pl.BlockSpec((1, LANE), lambda b: (0, 0)),
                  pl.BlockSpec((LANE, 1), lambda b: (0, 0)),
                  pl.BlockSpec((1, N_HEADS, P), lambda b: (b, 0, 0))],
        out_specs=[pl.BlockSpec((L, MIX), lambda b: (b, 0)),
                   pl.BlockSpec((L, LANE), lambda b: (b, 0))],
        out_shape=[jax.ShapeDtypeStruct((Bs * L, MIX), BF16),
                   jax.ShapeDtypeStruct((Bs * L, LANE), F32)],
        compiler_params=_params("parallel"),
        name="fox_sample",
    )(q, kn, vn, ck, cv, fg, fgt, bf_pad, bft, rsum)


def _gdn_prep_kernel(ab_ref, alog_ref, dtb_ref, gc_ref, beta_ref, gr_ref, *, pb, rb, C):
    a = ab_ref[:, :LANE]
    bt = ab_ref[:, LANE:]
    g = -jnp.exp(alog_ref[...]) * _softplus(a + dtb_ref[...])
    beta_ref[...] = _sigmoid(bt)
    r_i = _iota2((pb, pb), 0)
    c_i = _iota2((pb, pb), 1)
    same_chunk_tril = jnp.logical_and(r_i // C == c_i // C, c_i <= r_i).astype(F32)
    G = _dot_hi(same_chunk_tril, g)
    gc_ref[...] = G
    eye = (_iota2((LANE, LANE), 0) == _iota2((LANE, LANE), 1)).astype(F32)
    for c in range(pb // rb):
        gr_ref[0, c] = _dot_nt_hi(eye, G[c * rb:(c + 1) * rb])


def gdn_prep(ab, alog_pad, dtb_pad, B, T, rb, C):
    pb = min(T, max(rb, 256))
    nb = T // pb
    return pl.pallas_call(
        functools.partial(_gdn_prep_kernel, pb=pb, rb=rb, C=C),
        grid=(B, nb),
        in_specs=[pl.BlockSpec((pb, 2 * LANE), lambda b, i: (b * nb + i, 0)),
                  pl.BlockSpec((1, LANE), lambda b, i: (0, 0)),
                  pl.BlockSpec((1, LANE), lambda b, i: (0, 0))],
        out_specs=[pl.BlockSpec((pb, LANE), lambda b, i: (b * nb + i, 0)),
                   pl.BlockSpec((pb, LANE), lambda b, i: (b * nb + i, 0)),
                   pl.BlockSpec((1, pb // rb, LANE, rb), lambda b, i: (b, i, 0, 0))],
        out_shape=[jax.ShapeDtypeStruct((B * T, LANE), F32),
                   jax.ShapeDtypeStruct((B * T, LANE), F32),
                   jax.ShapeDtypeStruct((B, T // rb, LANE, rb), F32)],
        compiler_params=_params("parallel", "parallel"),
        name="gdn_prep",
    )(ab, alog_pad, dtb_pad)


def _gdn_kernel(xq_ref, xk_ref, xv_ref, wq_ref, wk_ref, wv_ref, gc_ref, beta_ref, gr_ref, s0_ref,
                bq_ref, bk_ref, bv_ref, gain_ref,
                o_ref, sout_ref, cq_ref, ck_ref, cv_ref, S_ref, tail_ref, *, rb, C, nblk, hb):
    hblk = pl.program_id(1)
    i = pl.program_id(2)
    nc = rb // C
    W = hb * HEAD_DIM
    n_iter = C.bit_length() - 2

    @pl.when(i == 0)
    def _():
        S_ref[...] = s0_ref[...]
        tail_ref[...] = jnp.zeros_like(tail_ref)
        tail_ref[0, 5:8, :] = bq_ref[...]
        tail_ref[1, 5:8, :] = bk_ref[...]
        tail_ref[2, 5:8, :] = bv_ref[...]

    row8 = _iota2((8, W), 0)

    def conv_silu(x_ref, w_ref, idx):
        x = x_ref[...]
        w = w_ref[...]
        t8 = tail_ref[idx]
        y = x * w[3:4, :]
        for j in range(1, 4):
            xr = pltpu.roll(x, j, 0)
            head = jnp.where(row8 < j, pltpu.roll(t8, j, 0), xr[:8])
            xs = jnp.concatenate([head, xr[8:]], axis=0)
            y = y + xs * w[3 - j:4 - j, :]
        tail_ref[idx] = x[rb - 8:, :]
        return y * _sigmoid(y)

    qc_all = conv_silu(xq_ref, wq_ref, 0)
    kc_all = conv_silu(xk_ref, wk_ref, 1)
    vv_all = conv_silu(xv_ref, wv_ref, 2)

    r_i = _iota2((C, C), 0)
    c_i = _iota2((C, C), 1)
    incl = c_i <= r_i
    strict = c_i < r_i
    lane = _iota2((rb, LANE), 1)
    gc_blk = gc_ref[...]
    beta_blk = beta_ref[...]

    heads = []
    for hh in range(hb):
        hsl = slice(hh * HEAD_DIM, (hh + 1) * HEAD_DIM)
        qc = qc_all[:, hsl]
        kc = kc_all[:, hsl]
        qn = qc * lax.rsqrt(jnp.sum(qc * qc, axis=-1, keepdims=True) + EPS) * (HEAD_DIM ** -0.5)
        kn = kc * lax.rsqrt(jnp.sum(kc * kc, axis=-1, keepdims=True) + EPS)
        head_id = hblk * hb + hh
        sel = lane == head_id
        g_col = jnp.sum(jnp.where(sel, gc_blk, 0.0), axis=1, keepdims=True)
        b_col = jnp.sum(jnp.where(sel, beta_blk, 0.0), axis=1, keepdims=True)
        g_row = gr_ref[0, 0, pl.ds(head_id, 1), :]
        heads.append(dict(qn=qn, kn=kn, vv=vv_all[:, hsl], g_col=g_col, b_col=b_col, g_row=g_row,
                          eg_col=jnp.exp(g_col), S=S_ref[hh], outs=[]))

    units = []
    for c in range(nc):
        sl = slice(c * C, (c + 1) * C)
        for hd in heads:
            gi = hd["g_col"][sl]
            u_ = dict(hd=hd, gi=gi, bi=hd["b_col"][sl], egi=hd["eg_col"][sl],
                      k=hd["kn"][sl], q=hd["qn"][sl], v=hd["vv"][sl])
            u_["decay"] = jnp.exp(jnp.where(incl, gi - hd["g_row"][:, sl], -jnp.inf))
            units.append(u_)
    for u_ in units:
        u_["kq"] = _mm1(jnp.concatenate([u_["k"], u_["q"]], axis=0), u_["k"], NT_DIMS)
    for u_ in units:
        u_["P"] = -jnp.where(strict, u_["bi"] * u_["kq"][:C] * u_["decay"], 0.0)
        u_["qk"] = jnp.where(incl, u_["kq"][C:] * u_["decay"], 0.0)
        u_["X"] = jnp.concatenate([u_["v"] * u_["bi"], u_["k"] * (u_["bi"] * u_["egi"])], axis=1)
    for _ in range(n_iter):
        for u_ in units:
            x_hi, x_lo = _split_bf16(u_["X"])
            p16 = u_["P"].astype(BF16)
            u_["Y"] = jnp.dot(p16, jnp.concatenate([x_hi, x_lo, p16], axis=1), preferred_element_type=F32)
        for u_ in units:
            u_["X"] = u_["X"] + (u_["Y"][:, :2 * HEAD_DIM] + u_["Y"][:, 2 * HEAD_DIM:4 * HEAD_DIM])
            u_["P"] = u_["Y"][:, 4 * HEAD_DIM:]
    for u_ in units:
        u_["Y"] = _mm2r(u_["P"], u_["X"])
    for u_ in units:
        X = u_["X"] + u_["Y"]
        g_last = u_["gi"][C - 1:C, :]
        u_["u"] = X[:, :HEAD_DIM]
        u_["wq"] = jnp.concatenate([X[:, HEAD_DIM:], u_["q"] * u_["egi"]], axis=0)
        u_["qk_kt"] = jnp.concatenate([u_["qk"], (u_["k"] * jnp.exp(g_last - u_["gi"])).T], axis=0)
        u_["g_last"] = jnp.exp(g_last)

    for c in range(nc):
        cu = units[c * hb:(c + 1) * hb]
        for u_ in cu:
            u_["ws_qs"] = _mm1(u_["wq"], u_["hd"]["S"])
        for u_ in cu:
            u_["v_new"] = u_["u"] - u_["ws_qs"][:C]
            u_["od"] = _mm2r(u_["qk_kt"], u_["v_new"])
        for u_ in cu:
            hd = u_["hd"]
            hd["outs"].append(u_["ws_qs"][C:] + u_["od"][:C])
            hd["S"] = hd["S"] * u_["g_last"] + u_["od"][C:]

    gain = gain_ref[...]
    for hh, hd in enumerate(heads):
        hsl = slice(hh * HEAD_DIM, (hh + 1) * HEAD_DIM)
        S_ref[hh] = hd["S"]
        o = hd["outs"][0] if nc == 1 else jnp.concatenate(hd["outs"], axis=0)
        y = o * lax.rsqrt(jnp.mean(o * o, axis=-1, keepdims=True) + EPS) * gain
        o_ref[:, hsl] = y.astype(BF16)

    @pl.when(i == nblk - 1)
    def _():
        for hh, hd in enumerate(heads):
            sout_ref[0, hh] = hd["S"]
        cq_ref[0] = xq_ref[pl.ds(rb - 3, 3), :]
        ck_ref[0] = xk_ref[pl.ds(rb - 3, 3), :]
        cv_ref[0] = xv_ref[pl.ds(rb - 3, 3), :]


def gdn_mixer(qkv, conv_w, gc, beta, gr, S0, buf0, layer, out_gain, B, T, rb, C, hb):
    nblk = T // rb
    H = N_HEADS
    W = hb * HEAD_DIM
    nhb = H // hb
    x_spec = lambda off: pl.BlockSpec((rb, W), lambda b, h, i: (b * nblk + i, off + h))
    w_spec = lambda off: pl.BlockSpec((4, W), lambda b, h, i: (0, off + h))
    b_spec = lambda off: pl.BlockSpec((None, None, 3, W), lambda b, h, i: (layer, b, 0, off + h))
    col_spec = pl.BlockSpec((rb, LANE), lambda b, h, i: (b * nblk + i, 0))
    s_spec = pl.BlockSpec((1, hb, HEAD_DIM, HEAD_DIM), lambda b, h, i: (b, h, 0, 0))
    c_out = pl.BlockSpec((1, 3, W), lambda b, h, i: (b, 0, h))
    mix, S, cq, ck, cv = pl.pallas_call(
        functools.partial(_gdn_kernel, rb=rb, C=C, nblk=nblk, hb=hb),
        grid=(B, nhb, nblk),
        in_specs=[x_spec(0), x_spec(nhb), x_spec(2 * nhb), w_spec(0), w_spec(nhb), w_spec(2 * nhb),
                  col_spec, col_spec,
                  pl.BlockSpec((1, 1, LANE, rb), lambda b, h, i: (b, i, 0, 0)),
                  pl.BlockSpec((None, None, hb, HEAD_DIM, HEAD_DIM), lambda b, h, i: (layer, b, h, 0, 0)),
                  b_spec(0), b_spec(nhb), b_spec(2 * nhb),
                  pl.BlockSpec((1, HEAD_DIM), lambda b, h, i: (0, 0))],
        out_specs=[pl.BlockSpec((rb, W), lambda b, h, i: (b * nblk + i, h)),
                   s_spec, c_out, c_out, c_out],
        out_shape=[jax.ShapeDtypeStruct((B * T, MIX), BF16),
                   jax.ShapeDtypeStruct((B, H, HEAD_DIM, HEAD_DIM), F32),
                   jax.ShapeDtypeStruct((B, 3, MIX), F32),
                   jax.ShapeDtypeStruct((B, 3, MIX), F32),
                   jax.ShapeDtypeStruct((B, 3, MIX), F32)],
        scratch_shapes=[pltpu.VMEM((hb, HEAD_DIM, HEAD_DIM), F32),
                        pltpu.VMEM((3, 8, W), F32)],
        compiler_params=_params("parallel", "parallel", "arbitrary"),
        name="gdn_mixer",
    )(qkv, qkv, qkv, conv_w, conv_w, conv_w, gc, beta, gr, S0, buf0, buf0, buf0, out_gain.reshape(1, HEAD_DIM))
    return mix, S, jnp.concatenate([cq, ck, cv], axis=-1)


def _pad_vec(v, n):
    return jnp.pad(v.astype(F32), (0, n - v.shape[0])).reshape(1, n)


def kernel(x_prompt, x_sample, mem_prompt, cache_fox_k, cache_fox_v, cache_fox_logf, state_gdn_S, state_gdn_conv, cache_mem_k, cache_mem_v, norm_gain, mem_norm_gain, w_mem_kv, w_in_fox, b_forget, w_in_gdn, gdn_conv_w, gdn_A_log, gdn_dt_bias, gdn_out_norm, w_out, final_norm):
    B, T, D = x_prompt.shape
    Bs, L, _ = x_sample.shape
    depth = norm_gain.shape[0]
    NM = mem_prompt.shape[1]
    P = cache_fox_k.shape[2]
    H = N_HEADS
    scale = HEAD_DIM ** -0.5
    QKV = 3 * MIX

    xp = x_prompt.reshape(B * T, D)
    xs = x_sample.reshape(Bs * L, D)
    mem = mem_prompt.reshape(B * NM, D)

    tq = min(T, 512)
    tk = min(T, 1024)
    n_fox = (depth + 1) // 2
    pk_all = pv_all = jnp.zeros((n_fox, B, H, T, HEAD_DIM), F32)
    sk_all = sv_all = jnp.zeros((n_fox, Bs, H, L, HEAD_DIM), F32)
    rb_p = min(T, CHUNK)
    head_major = (0, 1, 3, 2, 4)
    ck_heads = jnp.transpose(cache_fox_k, head_major)
    cv_heads = jnp.transpose(cache_fox_v, head_major)

    p_lf, p_S, p_conv, p_mk, p_mv = [], [], [], [], []
    s_lf, s_S, s_conv = [], [], []

    for i in range(depth):
        j = i // 2
        hm = rms_norm_rows(mem, mem_norm_gain[i], BF16)
        wkv = w_mem_kv[i].astype(BF16)
        mk_p = mm(hm, wkv[:, :MEMW], "f32").reshape(B, NM, MEMW)
        mv_p = mm(hm, wkv[:, MEMW:], "f32").reshape(B, NM, MEMW)
        p_mk.append(mk_p)
        p_mv.append(mv_p)

        if i == 0:
            hp = rms_norm_rows(xp, norm_gain[0], BF16)
            hs = rms_norm_rows(xs, norm_gain[0], BF16)
        w_o = w_out[i].astype(BF16)

        if i % 2 == 0:
            wq, wk, wv, wf, wqm, wz = cast_cols(
                w_in_fox, j, [(0, MIX), (MIX, MIX), (2 * MIX, MIX), (QKV, H), (QKV + H, MEMW),
                              (QKV + H + MEMW, MIX + MEMW)])
            bf_pad = _pad_vec(b_forget[j], LANE)

            q = mm(hp, wq, "scale_bf16", scale * LOG2E)
            pk_all, k16 = proj_kv(hp, wk, pk_all, j, n_fox, B, T)
            pv_all, v16 = proj_kv(hp, wv, pv_all, j, n_fox, B, T)
            fg = mm(hp, wf, "f32")
            qm_p = mm(hp, wqm, "scale_bf16", scale)
            g_p = mm(hp, wz, "silu_bf16")
            logf, qa, ka = fox_prep(fg, bf_pad, B, T, min(T, 512))
            mix_p = fox_attn(q, qa, k16, ka, v16, B, T, tq, tk)
            p_lf.append(logf[:, :H].reshape(B, T, H))

            q = mm(hs, wq, "scale_bf16", scale)
            sk_all, k16 = proj_kv(hs, wk, sk_all, j, n_fox, Bs, L)
            sv_all, v16 = proj_kv(hs, wv, sv_all, j, n_fox, Bs, L)
            fg = mm(hs, wf, "f32")
            qm_s = mm(hs, wqm, "scale_bf16", scale)
            g_s = mm(hs, wz, "silu_bf16")
            fgt = fg.reshape(Bs, L, LANE).transpose(0, 2, 1)
            clf_t = cache_fox_logf[j].astype(F32).transpose(0, 2, 1).reshape(Bs * H, P)
            rsum = suffix_sum(clf_t).reshape(Bs, H, P)
            mix_s, logf_s = fox_sample(q, k16, v16, ck_heads, cv_heads, j,
                                       fg, fgt, bf_pad, bf_pad.reshape(LANE, 1), rsum, Bs, L)
            s_lf.append(logf_s[:, :H].reshape(Bs, L, H))
        else:
            wqkv, wa, wb, wqm, wz = cast_cols(
                w_in_gdn, j, [(0, QKV), (QKV, H), (QKV + H, H), (QKV + 2 * H, MEMW),
                              (QKV + 2 * H + MEMW, MIX + MEMW)])
            wab = jnp.concatenate([wa, wb], axis=1)
            alog_pad = _pad_vec(gdn_A_log[j], LANE)
            dtb_pad = _pad_vec(gdn_dt_bias[j], LANE)

            qkv = mm(hp, wqkv, "f32")
            ab = mm(hp, wab, "f32")
            qm_p = mm(hp, wqm, "scale_bf16", scale)
            g_p = mm(hp, wz, "silu_bf16")
            gc, beta, gr = gdn_prep(ab, alog_pad, dtb_pad, B, T, rb_p, min(CHUNK, T))
            mix_p, S, conv = gdn_mixer(qkv, gdn_conv_w[j], gc, beta, gr,
                                       jnp.zeros((1, B, H, HEAD_DIM, HEAD_DIM), F32),
                                       jnp.zeros((1, B, 3, QKV), F32), 0,
                                       gdn_out_norm[j], B, T, rb_p, min(CHUNK, T), H)
            p_S.append(S)
            p_conv.append(conv)

            qkv = mm(hs, wqkv, "f32")
            ab = mm(hs, wab, "f32")
            qm_s = mm(hs, wqm, "scale_bf16", scale)
            g_s = mm(hs, wz, "silu_bf16")
            gc, beta, gr = gdn_prep(ab, alog_pad, dtb_pad, Bs, L, L, L)
            mix_s, S, conv = gdn_mixer(qkv, gdn_conv_w[j], gc, beta, gr, state_gdn_S, state_gdn_conv, j,
                                       gdn_out_norm[j], Bs, L, L, L, H)
            s_S.append(S)
            s_conv.append(conv)

        m_p = mem_attn(qm_p, mk_p, mv_p, T)
        m_s = mem_attn(qm_s, cache_mem_k, cache_mem_v, L, layer=i)
        if i + 1 < depth:
            xp, hp = out_proj(mix_p, m_p, g_p, w_o, xp, norm_gain[i + 1], False)
            xs, hs = out_proj(mix_s, m_s, g_s, w_o, xs, norm_gain[i + 1], False)
        else:
            y_prompt = out_proj(mix_p, m_p, g_p, w_o, xp, final_norm, True).reshape(B, T, D)
            y_sample = out_proj(mix_s, m_s, g_s, w_o, xs, final_norm, True).reshape(Bs, L, D)

    mem_shape = (depth, B, NM, N_MEM_HEADS, HEAD_DIM)
    return (y_prompt, y_sample,
            jnp.transpose(pk_all, head_major), jnp.transpose(pv_all, head_major), jnp.stack(p_lf), jnp.stack(p_S), jnp.stack(p_conv),
            jnp.stack(p_mk).reshape(mem_shape), jnp.stack(p_mv).reshape(mem_shape),
            jnp.transpose(sk_all, head_major), jnp.transpose(sv_all, head_major), jnp.stack(s_lf), jnp.stack(s_S), jnp.stack(s_conv))
```

```python
import functools
import math

import jax
import jax.numpy as jnp
from jax import lax
from jax.experimental import pallas as pl
from jax.experimental.pallas import tpu as pltpu

F32 = jnp.float32
BF16 = jnp.bfloat16
HI = lax.Precision.HIGHEST

LANE = 128
HEAD_DIM = 128
N_HEADS = 12
MIX = N_HEADS * HEAD_DIM
N_MEM_HEADS = 4
MEMW = N_MEM_HEADS * HEAD_DIM
CHUNK = 64
EPS = 1e-6
LOG2E = math.log2(math.e)
VMEM_LIMIT = 48 * 1024 * 1024

NN_DIMS = (((1,), (0,)), ((), ()))
NT_DIMS = (((1,), (1,)), ((), ()))


def _params(*sem):
    return pltpu.CompilerParams(dimension_semantics=sem, vmem_limit_bytes=VMEM_LIMIT)


def _sigmoid(x):
    return 1.0 / (1.0 + jnp.exp(-x))


def _log_sigmoid(x):
    return jnp.minimum(x, 0.0) - jnp.log1p(jnp.exp(-jnp.abs(x)))


def _softplus(x):
    return jnp.maximum(x, 0.0) + jnp.log1p(jnp.exp(-jnp.abs(x)))


def _dot_hi(a, b):
    return jnp.dot(a, b, precision=HI, preferred_element_type=F32)


def _dot_nt_hi(a, b):
    return lax.dot_general(a, b, NT_DIMS, precision=HI, preferred_element_type=F32)


def _split_bf16(a):
    hi = a.astype(BF16)
    lo = (a - hi.astype(F32)).astype(BF16)
    return hi, lo


def _mm1(a, b, dims=NN_DIMS):
    return lax.dot_general(a.astype(BF16), b.astype(BF16), dims, preferred_element_type=F32)


def _mm2r(a, b):
    b_hi, b_lo = _split_bf16(b)
    aa = a.astype(BF16)
    return (jnp.dot(aa, b_hi, preferred_element_type=F32)
            + jnp.dot(aa, b_lo, preferred_element_type=F32))


def _iota2(shape, dim):
    return lax.broadcasted_iota(jnp.int32, shape, dim)


def _rms_kernel(x_ref, g_ref, o_ref):
    x = x_ref[...]
    y = x * lax.rsqrt(jnp.mean(x * x, axis=-1, keepdims=True) + EPS)
    o_ref[...] = (y * g_ref[...]).astype(o_ref.dtype)


def rms_norm_rows(x, gain, out_dtype):
    R, D = x.shape
    tm = min(R, 512)
    return pl.pallas_call(
        _rms_kernel,
        grid=(R // tm,),
        in_specs=[pl.BlockSpec((tm, D), lambda i: (i, 0)),
                  pl.BlockSpec((1, D), lambda i: (0, 0))],
        out_specs=pl.BlockSpec((tm, D), lambda i: (i, 0)),
        out_shape=jax.ShapeDtypeStruct((R, D), out_dtype),
        compiler_params=_params("parallel"),
        name="rms_norm",
    )(x, gain.reshape(1, D))


def _cast_cols_kernel(w_ref, *o_refs, cols):
    for o_ref, (start, width) in zip(o_refs, cols):
        x = w_ref[:, start:start + width].astype(BF16)
        pad = o_ref.shape[1] - width
        if pad:
            x = jnp.concatenate([x, jnp.zeros((x.shape[0], pad), BF16)], axis=1)
        o_ref[...] = x


def cast_cols(w_all, layer, cols):
    _, K, N = w_all.shape
    tkb = min(K, 256)
    widths = [-(-width // LANE) * LANE for _, width in cols]
    return pl.pallas_call(
        functools.partial(_cast_cols_kernel, cols=tuple(cols)),
        grid=(K // tkb,),
        in_specs=[pl.BlockSpec((None, tkb, N), lambda i: (layer, i, 0))],
        out_specs=[pl.BlockSpec((tkb, wd), lambda i: (i, 0)) for wd in widths],
        out_shape=[jax.ShapeDtypeStruct((K, wd), BF16) for wd in widths],
        compiler_params=_params("parallel"),
        name="cast_cols",
    )(w_all)


def _mm_kernel(h_ref, w_ref, *o_refs, kind, scale):
    acc = jnp.dot(h_ref[...], w_ref[...], preferred_element_type=F32)
    if kind == "f32":
        o_refs[0][...] = acc
    elif kind == "scale_bf16":
        o_refs[0][...] = (acc * scale).astype(BF16)
    elif kind == "silu_bf16":
        o_refs[0][...] = (acc * _sigmoid(acc)).astype(BF16)
    else:
        raise ValueError(kind)


def mm(h, w, kind, scale=1.0):
    R, K = h.shape
    N = w.shape[1]
    tm = min(R, 1024)
    tn = min(N, 512)
    out_dtypes = {"f32": [F32], "scale_bf16": [BF16], "silu_bf16": [BF16]}[kind]
    outs = pl.pallas_call(
        functools.partial(_mm_kernel, kind=kind, scale=scale),
        grid=(R // tm, N // tn),
        in_specs=[pl.BlockSpec((tm, K), lambda i, n: (i, 0)),
                  pl.BlockSpec((K, tn), lambda i, n: (0, n))],
        out_specs=[pl.BlockSpec((tm, tn), lambda i, n: (i, n)) for _ in out_dtypes],
        out_shape=[jax.ShapeDtypeStruct((R, N), dt) for dt in out_dtypes],
        compiler_params=_params("parallel", "arbitrary"),
        name="proj_" + kind,
    )(h, w)
    return outs if len(outs) > 1 else outs[0]


def _proj_kv_kernel(h_ref, w_ref, stacked_ref, o32_ref, o16_ref):
    del stacked_ref
    nb, _, tt, _ = o32_ref.shape
    acc = jnp.dot(h_ref[...], w_ref[...], preferred_element_type=F32)
    o16_ref[...] = acc.astype(BF16)
    o32_ref[:, 0] = acc[:, :HEAD_DIM].reshape(nb, tt, HEAD_DIM)
    o32_ref[:, 1] = acc[:, HEAD_DIM:].reshape(nb, tt, HEAD_DIM)


def proj_kv(h, w, stacked, layer, n_layers, NB, T):
    R, K = h.shape
    tm = min(R, 1024)
    tt = min(tm, T)
    nb = tm // tt
    nt = T // tt
    return pl.pallas_call(
        _proj_kv_kernel,
        grid=(R // tm, N_HEADS // 2),
        in_specs=[pl.BlockSpec((tm, K), lambda i, hp: (i, 0)),
                  pl.BlockSpec((K, 2 * HEAD_DIM), lambda i, hp: (0, hp)),
                  pl.BlockSpec(memory_space=pl.ANY)],
        out_specs=[pl.BlockSpec((None, nb, 2, tt, HEAD_DIM), lambda i, hp: (layer, i // nt, hp, i % nt, 0)),
                   pl.BlockSpec((tm, 2 * HEAD_DIM), lambda i, hp: (i, hp))],
        out_shape=[jax.ShapeDtypeStruct((n_layers, NB, N_HEADS, T, HEAD_DIM), F32),
                   jax.ShapeDtypeStruct((R, MIX), BF16)],
        input_output_aliases={2: 0},
        compiler_params=_params("parallel", "arbitrary"),
        name="proj_kv",
    )(h, w, stacked)


def _out_proj_kernel(mix_ref, m_ref, g_ref, w_ref, x_ref, gain_ref, *o_refs):
    g = g_ref[...].astype(F32)
    br = jnp.concatenate([(mix_ref[...].astype(F32) * g[:, :MIX]).astype(BF16),
                          (m_ref[...].astype(F32) * g[:, MIX:]).astype(BF16)], axis=1)
    x = x_ref[...] + jnp.dot(br, w_ref[...], preferred_element_type=F32)
    y = x * lax.rsqrt(jnp.mean(x * x, axis=-1, keepdims=True) + EPS) * gain_ref[...]
    o_refs[-1][...] = y.astype(o_refs[-1].dtype)
    if len(o_refs) == 2:
        o_refs[0][...] = x


def out_proj(mix, m, g, w, x, gain, last):
    R, D = x.shape
    BW = MIX + MEMW
    tm = min(R, 256)
    row = lambda width: pl.BlockSpec((tm, width), lambda i: (i, 0))
    if last:
        out_specs, out_shape = [row(D)], [jax.ShapeDtypeStruct((R, D), F32)]
    else:
        out_specs = [row(D), row(D)]
        out_shape = [jax.ShapeDtypeStruct((R, D), F32), jax.ShapeDtypeStruct((R, D), BF16)]
    outs = pl.pallas_call(
        _out_proj_kernel,
        grid=(R // tm,),
        in_specs=[row(MIX), row(MEMW), row(BW),
                  pl.BlockSpec((BW, D), lambda i: (0, 0)),
                  row(D),
                  pl.BlockSpec((1, D), lambda i: (0, 0))],
        out_specs=out_specs,
        out_shape=out_shape,
        compiler_params=_params("parallel"),
        name="out_proj",
    )(mix, m, g, w, x, gain.reshape(1, D))
    return outs[0] if last else outs


def _mem_attn_kernel(q_ref, k_ref, v_ref, o_ref, *, per_head_kv):
    for h in range(N_MEM_HEADS):
        sl = slice(h * HEAD_DIM, (h + 1) * HEAD_DIM)
        q = q_ref[:, sl]
        if per_head_kv:
            k = k_ref[:, h, :].astype(BF16)
            v = v_ref[:, h, :].astype(BF16)
        else:
            k = k_ref[:, sl].astype(BF16)
            v = v_ref[:, sl].astype(BF16)
        s = lax.dot_general(q, k, NT_DIMS, preferred_element_type=F32)
        m = jnp.max(s, axis=1, keepdims=True)
        p = jnp.exp(s - m)
        l = jnp.sum(p, axis=1, keepdims=True)
        o = jnp.dot(p.astype(BF16), v, preferred_element_type=F32) / l
        o_ref[:, sl] = o.astype(BF16)


def mem_attn(q, mk, mv, rows_per_batch, layer=None):
    R = q.shape[0]
    per_head_kv = layer is not None
    tr = min(rows_per_batch, 1024)
    nrb = rows_per_batch // tr
    if per_head_kv:
        NB, NM = mk.shape[1:3]
        kv_spec = pl.BlockSpec((None, None, NM, N_MEM_HEADS, HEAD_DIM), lambda b, i: (layer, b, 0, 0, 0))
    else:
        NB, NM = mk.shape[:2]
        kv_spec = pl.BlockSpec((None, NM, MEMW), lambda b, i: (b, 0, 0))
    return pl.pallas_call(
        functools.partial(_mem_attn_kernel, per_head_kv=per_head_kv),
        grid=(NB, nrb),
        in_specs=[pl.BlockSpec((tr, MEMW), lambda b, i: (b * nrb + i, 0)), kv_spec, kv_spec],
        out_specs=pl.BlockSpec((tr, MEMW), lambda b, i: (b * nrb + i, 0)),
        out_shape=jax.ShapeDtypeStruct((R, MEMW), BF16),
        compiler_params=_params("parallel", "parallel"),
        name="mem_attn",
    )(q, mk, mv)


def _fox_prep_kernel(fg_ref, bf_ref, logf_ref, qa_ref, ka_ref, carry_ref, *, tp):
    @pl.when(pl.program_id(1) == 0)
    def _():
        carry_ref[...] = jnp.zeros_like(carry_ref)

    logf = _log_sigmoid(fg_ref[...] + bf_ref[...])
    tril = (_iota2((tp, tp), 1) <= _iota2((tp, tp), 0)).astype(F32)
    F = _dot_hi(tril, logf) + carry_ref[...]
    carry_ref[...] = F[tp - 1:tp, :]
    logf_ref[...] = logf

    F2 = F * LOG2E
    hi = F2.astype(BF16)
    r1 = F2 - hi.astype(F32)
    mid = r1.astype(BF16)
    lo = (r1 - mid.astype(F32)).astype(BF16)
    col = _iota2((LANE, MIX), 1)
    own_head = _iota2((LANE, MIX), 0) == (col >> 7)
    slot = col & (LANE - 1)

    def spread(x, c):
        sel = jnp.logical_and(own_head, slot == c).astype(BF16)
        return jnp.dot(x, sel, preferred_element_type=F32)

    slot_row = _iota2((1, MIX), 1) & (LANE - 1)
    ones_q = jnp.logical_and(slot_row >= 3, slot_row < 6).astype(F32)
    ones_k = (slot_row < 3).astype(F32)
    qa_ref[...] = (spread(hi, 0) + spread(mid, 1) + spread(lo, 2) + ones_q).astype(BF16)
    ka_ref[...] = (ones_k - (spread(hi, 3) + spread(mid, 4) + spread(lo, 5))).astype(BF16)


def fox_prep(fg, bf_pad, B, T, tp):
    nb = T // tp
    return pl.pallas_call(
        functools.partial(_fox_prep_kernel, tp=tp),
        grid=(B, nb),
        in_specs=[pl.BlockSpec((tp, LANE), lambda b, i: (b * nb + i, 0)),
                  pl.BlockSpec((1, LANE), lambda b, i: (0, 0))],
        out_specs=[pl.BlockSpec((tp, LANE), lambda b, i: (b * nb + i, 0)),
                   pl.BlockSpec((tp, MIX), lambda b, i: (b * nb + i, 0)),
                   pl.BlockSpec((tp, MIX), lambda b, i: (b * nb + i, 0))],
        out_shape=[jax.ShapeDtypeStruct((B * T, LANE), F32),
                   jax.ShapeDtypeStruct((B * T, MIX), BF16),
                   jax.ShapeDtypeStruct((B * T, MIX), BF16)],
        scratch_shapes=[pltpu.VMEM((1, LANE), F32)],
        compiler_params=_params("parallel", "arbitrary"),
        name="fox_prep",
    )(fg, bf_pad)


def _fox_attn_kernel(q_ref, qa_ref, k_ref, ka_ref, v_ref, o_ref, *, tq, tk, n_sub):
    qi = pl.program_id(2)
    sub = tq // n_sub
    qc = jnp.concatenate([q_ref[...], qa_ref[...]], axis=1)
    col = _iota2((sub, tk), 1)
    rows = [qi * tq + s * sub + _iota2((sub, tk), 0) for s in range(n_sub)]

    def body(kj, stats, masked):
        off = pl.multiple_of(kj * tk, tk)
        kc = jnp.concatenate([k_ref[pl.ds(off, tk), :], ka_ref[pl.ds(off, tk), :]], axis=1)
        vb = v_ref[pl.ds(off, tk), :]
        sc_all = lax.dot_general(qc, kc, NT_DIMS, preferred_element_type=F32)
        new = []
        for s in range(n_sub):
            m, l, acc = stats[s]
            sc = sc_all[s * sub:(s + 1) * sub, :]
            if masked:
                sc = jnp.where(col + kj * tk <= rows[s], sc, -jnp.inf)
            m_new = jnp.maximum(m, jnp.max(sc, axis=1, keepdims=True))
            alpha = jnp.exp2(m - m_new)
            p = jnp.exp2(sc - m_new)
            l = alpha * l + jnp.sum(p, axis=1, keepdims=True)
            acc = alpha * acc + jnp.dot(p.astype(BF16), vb, preferred_element_type=F32)
            new.append((m_new, l, acc))
        return tuple(new)

    n_full = (qi * tq + 1) // tk
    n_all = ((qi + 1) * tq + tk - 1) // tk
    stats = tuple((jnp.full((sub, 1), -jnp.inf, F32), jnp.zeros((sub, 1), F32), jnp.zeros((sub, HEAD_DIM), F32))
                  for _ in range(n_sub))
    stats = lax.fori_loop(0, n_full, functools.partial(body, masked=False), stats)
    stats = lax.fori_loop(n_full, n_all, functools.partial(body, masked=True), stats)
    for s in range(n_sub):
        m, l, acc = stats[s]
        o_ref[s * sub:(s + 1) * sub, :] = (acc / l).astype(BF16)


def fox_attn(q, qa, k, ka, v, B, T, tq, tk):
    nq = T // tq
    n_sub = 2 if tq >= 512 else 1
    q_spec = pl.BlockSpec((tq, HEAD_DIM), lambda b, h, i: (b * nq + i, h))
    kv_spec = pl.BlockSpec((T, HEAD_DIM), lambda b, h, i: (b, h))
    return pl.pallas_call(
        functools.partial(_fox_attn_kernel, tq=tq, tk=tk, n_sub=n_sub),
        grid=(B, N_HEADS, nq),
        in_specs=[q_spec, q_spec, kv_spec, kv_spec, kv_spec],
        out_specs=pl.BlockSpec((tq, HEAD_DIM), lambda b, h, i: (b * nq + i, h)),
        out_shape=jax.ShapeDtypeStruct((B * T, MIX), BF16),
        compiler_params=_params("parallel", "parallel", "parallel"),
        name="fox_attn",
    )(q, qa, k, ka, v)


def _suffix_sum_kernel(x_ref, o_ref):
    P = x_ref.shape[1]
    after = (_iota2((P, P), 0) > _iota2((P, P), 1)).astype(F32)
    o_ref[...] = _dot_hi(x_ref[...], after)


def suffix_sum(x):
    return pl.pallas_call(
        _suffix_sum_kernel,
        out_shape=jax.ShapeDtypeStruct(x.shape, F32),
        compiler_params=pltpu.CompilerParams(vmem_limit_bytes=VMEM_LIMIT),
        name="fox_suffix_sum",
    )(x)


def _fox_sample_kernel(q_ref, kn_ref, vn_ref, ck_ref, cv_ref, fg_ref, fgt_ref, bf_ref, bft_ref, r_ref,
                       o_ref, logf_ref, *, L):
    logf = _log_sigmoid(fg_ref[...] + bf_ref[...])
    logf_ref[...] = logf
    r_i = _iota2((L, L), 0)
    c_i = _iota2((L, L), 1)
    causal = c_i <= r_i
    ln_col = _dot_hi(causal.astype(F32), logf)
    logf_t = _log_sigmoid(fgt_ref[0] + bft_ref[...])
    ln_row = _dot_hi(logf_t, (r_i <= c_i).astype(F32))
    for h in range(N_HEADS):
        sl = slice(h * HEAD_DIM, (h + 1) * HEAD_DIM)
        q = q_ref[:, sl]
        kc = ck_ref[h].astype(BF16)
        vc = cv_ref[h].astype(BF16)
        fq = ln_col[:, h:h + 1]
        sc = lax.dot_general(q, kc, NT_DIMS, preferred_element_type=F32) + (fq + r_ref[0, h:h + 1, :])
        sn = lax.dot_general(q, kn_ref[:, sl], NT_DIMS, preferred_element_type=F32) + (fq - ln_row[h:h + 1, :])
        sn = jnp.where(causal, sn, -jnp.inf)
        m = jnp.maximum(jnp.max(sc, axis=1, keepdims=True), jnp.max(sn, axis=1, keepdims=True))
        pc = jnp.exp(sc - m)
        pn = jnp.exp(sn - m)
        l = jnp.sum(pc, axis=1, keepdims=True) + jnp.sum(pn, axis=1, keepdims=True)
        o = (jnp.dot(pc.astype(BF16), vc, preferred_element_type=F32)
             + jnp.dot(pn.astype(BF16), vn_ref[:, sl], preferred_element_type=F32)) / l
        o_ref[:, sl] = o.astype(BF16)


def fox_sample(q, kn, vn, ck, cv, layer, fg, fgt, bf_pad, bft, rsum, Bs, L):
    P = ck.shape[3]
    cache_spec = pl.BlockSpec((None, None, N_HEADS, P, HEAD_DIM), lambda b: (layer, b, 0, 0, 0))
    return pl.pallas_call(
        functools.partial(_fox_sample_kernel, L=L),
        grid=(Bs,),
        in_specs=[pl.BlockSpec((L, MIX), lambda b: (b, 0)),
                  pl.BlockSpec((L, MIX), lambda b: (b, 0)),
                  pl.BlockSpec((L, MIX), lambda b: (b, 0)),
                  cache_spec, cache_spec,
                  pl.BlockSpec((L, LANE), lambda b: (b, 0)),
                  pl.BlockSpec((1, LANE, L), lambda b: (b, 0, 0)),
                  pl.BlockSpec((1, LANE), lambda b: (0, 0)),
                  pl.BlockSpec((LANE, 1), lambda b: (0, 0)),
                  pl.BlockSpec((1, N_HEADS, P), lambda b: (b, 0, 0))],
        out_specs=[pl.BlockSpec((L, MIX), lambda b: (b, 0)),
                   pl.BlockSpec((L, LANE), lambda b: (b, 0))],
        out_shape=[jax.ShapeDtypeStruct((Bs * L, MIX), BF16),
                   jax.ShapeDtypeStruct((Bs * L, LANE), F32)],
        compiler_params=_params("parallel"),
        name="fox_sample",
    )(q, kn, vn, ck, cv, fg, fgt, bf_pad, bft, rsum)


def _gdn_prep_kernel(ab_ref, alog_ref, dtb_ref, gc_ref, beta_ref, gr_ref, *, pb, rb, C):
    a = ab_ref[:, :LANE]
    bt = ab_ref[:, LANE:]
    g = -jnp.exp(alog_ref[...]) * _softplus(a + dtb_ref[...])
    beta_ref[...] = _sigmoid(bt)
    r_i = _iota2((pb, pb), 0)
    c_i = _iota2((pb, pb), 1)
    same_chunk_tril = jnp.logical_and(r_i // C == c_i // C, c_i <= r_i).astype(F32)
    G = _dot_hi(same_chunk_tril, g)
    gc_ref[...] = G
    eye = (_iota2((LANE, LANE), 0) == _iota2((LANE, LANE), 1)).astype(F32)
    for c in range(pb // rb):
        gr_ref[0, c] = _dot_nt_hi(eye, G[c * rb:(c + 1) * rb])


def gdn_prep(ab, alog_pad, dtb_pad, B, T, rb, C):
    pb = min(T, max(rb, 256))
    nb = T // pb
    return pl.pallas_call(
        functools.partial(_gdn_prep_kernel, pb=pb, rb=rb, C=C),
        grid=(B, nb),
        in_specs=[pl.BlockSpec((pb, 2 * LANE), lambda b, i: (b * nb + i, 0)),
                  pl.BlockSpec((1, LANE), lambda b, i: (0, 0)),
                  pl.BlockSpec((1, LANE), lambda b, i: (0, 0))],
        out_specs=[pl.BlockSpec((pb, LANE), lambda b, i: (b * nb + i, 0)),
                   pl.BlockSpec((pb, LANE), lambda b, i: (b * nb + i, 0)),
                   pl.BlockSpec((1, pb // rb, LANE, rb), lambda b, i: (b, i, 0, 0))],
        out_shape=[jax.ShapeDtypeStruct((B * T, LANE), F32),
                   jax.ShapeDtypeStruct((B * T, LANE), F32),
                   jax.ShapeDtypeStruct((B, T // rb, LANE, rb), F32)],
        compiler_params=_params("parallel", "parallel"),
        name="gdn_prep",
    )(ab, alog_pad, dtb_pad)


def _gdn_kernel(xq_ref, xk_ref, xv_ref, wq_ref, wk_ref, wv_ref, gc_ref, beta_ref, gr_ref, s0_ref,
                bq_ref, bk_ref, bv_ref, gain_ref,
                o_ref, sout_ref, cq_ref, ck_ref, cv_ref, S_ref, tail_ref, *, rb, C, nblk, hb):
    hblk = pl.program_id(1)
    i = pl.program_id(2)
    nc = rb // C
    W = hb * HEAD_DIM
    n_iter = C.bit_length() - 2

    @pl.when(i == 0)
    def _():
        S_ref[...] = s0_ref[...]
        tail_ref[...] = jnp.zeros_like(tail_ref)
        tail_ref[0, 5:8, :] = bq_ref[...]
        tail_ref[1, 5:8, :] = bk_ref[...]
        tail_ref[2, 5:8, :] = bv_ref[...]

    row8 = _iota2((8, W), 0)

    def conv_silu(x_ref, w_ref, idx):
        x = x_ref[...]
        w = w_ref[...]
        t8 = tail_ref[idx]
        y = x * w[3:4, :]
        for j in range(1, 4):
            xr = pltpu.roll(x, j, 0)
            head = jnp.where(row8 < j, pltpu.roll(t8, j, 0), xr[:8])
            xs = jnp.concatenate([head, xr[8:]], axis=0)
            y = y + xs * w[3 - j:4 - j, :]
        tail_ref[idx] = x[rb - 8:, :]
        return y * _sigmoid(y)

    qc_all = conv_silu(xq_ref, wq_ref, 0)
    kc_all = conv_silu(xk_ref, wk_ref, 1)
    vv_all = conv_silu(xv_ref, wv_ref, 2)

    r_i = _iota2((C, C), 0)
    c_i = _iota2((C, C), 1)
    incl = c_i <= r_i
    strict = c_i < r_i
    lane = _iota2((rb, LANE), 1)
    gc_blk = gc_ref[...]
    beta_blk = beta_ref[...]

    heads = []
    for hh in range(hb):
        hsl = slice(hh * HEAD_DIM, (hh + 1) * HEAD_DIM)
        qc = qc_all[:, hsl]
        kc = kc_all[:, hsl]
        qn = qc * lax.rsqrt(jnp.sum(qc * qc, axis=-1, keepdims=True) + EPS) * (HEAD_DIM ** -0.5)
        kn = kc * lax.rsqrt(jnp.sum(kc * kc, axis=-1, keepdims=True) + EPS)
        head_id = hblk * hb + hh
        sel = lane == head_id
        g_col = jnp.sum(jnp.where(sel, gc_blk, 0.0), axis=1, keepdims=True)
        b_col = jnp.sum(jnp.where(sel, beta_blk, 0.0), axis=1, keepdims=True)
        g_row = gr_ref[0, 0, pl.ds(head_id, 1), :]
        heads.append(dict(qn=qn, kn=kn, vv=vv_all[:, hsl], g_col=g_col, b_col=b_col, g_row=g_row,
                          eg_col=jnp.exp(g_col), S=S_ref[hh], outs=[]))

    units = []
    for c in range(nc):
        sl = slice(c * C, (c + 1) * C)
        for hd in heads:
            gi = hd["g_col"][sl]
            u_ = dict(hd=hd, gi=gi, bi=hd["b_col"][sl], egi=hd["eg_col"][sl],
                      k=hd["kn"][sl], q=hd["qn"][sl], v=hd["vv"][sl])
            u_["decay"] = jnp.exp(jnp.where(incl, gi - hd["g_row"][:, sl], -jnp.inf))
            units.append(u_)
    for u_ in units:
        u_["kq"] = _mm1(jnp.concatenate([u_["k"], u_["q"]], axis=0), u_["k"], NT_DIMS)
    for u_ in units:
        u_["P"] = -jnp.where(strict, u_["bi"] * u_["kq"][:C] * u_["decay"], 0.0)
        u_["qk"] = jnp.where(incl, u_["kq"][C:] * u_["decay"], 0.0)
        u_["X"] = jnp.concatenate([u_["v"] * u_["bi"], u_["k"] * (u_["bi"] * u_["egi"])], axis=1)
    for _ in range(n_iter):
        for u_ in units:
            x_hi, x_lo = _split_bf16(u_["X"])
            p16 = u_["P"].astype(BF16)
            u_["Y"] = jnp.dot(p16, jnp.concatenate([x_hi, x_lo, p16], axis=1), preferred_element_type=F32)
        for u_ in units:
            u_["X"] = u_["X"] + (u_["Y"][:, :2 * HEAD_DIM] + u_["Y"][:, 2 * HEAD_DIM:4 * HEAD_DIM])
            u_["P"] = u_["Y"][:, 4 * HEAD_DIM:]
    for u_ in units:
        u_["Y"] = _mm2r(u_["P"], u_["X"])
    for u_ in units:
        X = u_["X"] + u_["Y"]
        g_last = u_["gi"][C - 1:C, :]
        u_["u"] = X[:, :HEAD_DIM]
        u_["wq"] = jnp.concatenate([X[:, HEAD_DIM:], u_["q"] * u_["egi"]], axis=0)
        u_["qk_kt"] = jnp.concatenate([u_["qk"], (u_["k"] * jnp.exp(g_last - u_["gi"])).T], axis=0)
        u_["g_last"] = jnp.exp(g_last)

    for c in range(nc):
        cu = units[c * hb:(c + 1) * hb]
        for u_ in cu:
            u_["ws_qs"] = _mm1(u_["wq"], u_["hd"]["S"])
        for u_ in cu:
            u_["v_new"] = u_["u"] - u_["ws_qs"][:C]
            u_["od"] = _mm2r(u_["qk_kt"], u_["v_new"])
        for u_ in cu:
            hd = u_["hd"]
            hd["outs"].append(u_["ws_qs"][C:] + u_["od"][:C])
            hd["S"] = hd["S"] * u_["g_last"] + u_["od"][C:]

    gain = gain_ref[...]
    for hh, hd in enumerate(heads):
        hsl = slice(hh * HEAD_DIM, (hh + 1) * HEAD_DIM)
        S_ref[hh] = hd["S"]
        o = hd["outs"][0] if nc == 1 else jnp.concatenate(hd["outs"], axis=0)
        y = o * lax.rsqrt(jnp.mean(o * o, axis=-1, keepdims=True) + EPS) * gain
        o_ref[:, hsl] = y.astype(BF16)

    @pl.when(i == nblk - 1)
    def _():
        for hh, hd in enumerate(heads):
            sout_ref[0, hh] = hd["S"]
        cq_ref[0] = xq_ref[pl.ds(rb - 3, 3), :]
        ck_ref[0] = xk_ref[pl.ds(rb - 3, 3), :]
        cv_ref[0] = xv_ref[pl.ds(rb - 3, 3), :]


def gdn_mixer(qkv, conv_w, gc, beta, gr, S0, buf0, layer, out_gain, B, T, rb, C, hb):
    nblk = T // rb
    H = N_HEADS
    W = hb * HEAD_DIM
    nhb = H // hb
    x_spec = lambda off: pl.BlockSpec((rb, W), lambda b, h, i: (b * nblk + i, off + h))
    w_spec = lambda off: pl.BlockSpec((4, W), lambda b, h, i: (0, off + h))
    b_spec = lambda off: pl.BlockSpec((None, None, 3, W), lambda b, h, i: (layer, b, 0, off + h))
    col_spec = pl.BlockSpec((rb, LANE), lambda b, h, i: (b * nblk + i, 0))
    s_spec = pl.BlockSpec((1, hb, HEAD_DIM, HEAD_DIM), lambda b, h, i: (b, h, 0, 0))
    c_out = pl.BlockSpec((1, 3, W), lambda b, h, i: (b, 0, h))
    mix, S, cq, ck, cv = pl.pallas_call(
        functools.partial(_gdn_kernel, rb=rb, C=C, nblk=nblk, hb=hb),
        grid=(B, nhb, nblk),
        in_specs=[x_spec(0), x_spec(nhb), x_spec(2 * nhb), w_spec(0), w_spec(nhb), w_spec(2 * nhb),
                  col_spec, col_spec,
                  pl.BlockSpec((1, 1, LANE, rb), lambda b, h, i: (b, i, 0, 0)),
                  pl.BlockSpec((None, None, hb, HEAD_DIM, HEAD_DIM), lambda b, h, i: (layer, b, h, 0, 0)),
                  b_spec(0), b_spec(nhb), b_spec(2 * nhb),
                  pl.BlockSpec((1, HEAD_DIM), lambda b, h, i: (0, 0))],
        out_specs=[pl.BlockSpec((rb, W), lambda b, h, i: (b * nblk + i, h)),
                   s_spec, c_out, c_out, c_out],
        out_shape=[jax.ShapeDtypeStruct((B * T, MIX), BF16),
                   jax.ShapeDtypeStruct((B, H, HEAD_DIM, HEAD_DIM), F32),
                   jax.ShapeDtypeStruct((B, 3, MIX), F32),
                   jax.ShapeDtypeStruct((B, 3, MIX), F32),
                   jax.ShapeDtypeStruct((B, 3, MIX), F32)],
        scratch_shapes=[pltpu.VMEM((hb, HEAD_DIM, HEAD_DIM), F32),
                        pltpu.VMEM((3, 8, W), F32)],
        compiler_params=_params("parallel", "parallel", "arbitrary"),
        name="gdn_mixer",
    )(qkv, qkv, qkv, conv_w, conv_w, conv_w, gc, beta, gr, S0, buf0, buf0, buf0, out_gain.reshape(1, HEAD_DIM))
    return mix, S, jnp.concatenate([cq, ck, cv], axis=-1)


def _pad_vec(v, n):
    return jnp.pad(v.astype(F32), (0, n - v.shape[0])).reshape(1, n)


def kernel(x_prompt, x_sample, mem_prompt, cache_fox_k, cache_fox_v, cache_fox_logf, state_gdn_S, state_gdn_conv, cache_mem_k, cache_mem_v, norm_gain, mem_norm_gain, w_mem_kv, w_in_fox, b_forget, w_in_gdn, gdn_conv_w, gdn_A_log, gdn_dt_bias, gdn_out_norm, w_out, final_norm):
    B, T, D = x_prompt.shape
    Bs, L, _ = x_sample.shape
    depth = norm_gain.shape[0]
    NM = mem_prompt.shape[1]
    P = cache_fox_k.shape[2]
    H = N_HEADS
    scale = HEAD_DIM ** -0.5
    QKV = 3 * MIX

    xp = x_prompt.reshape(B * T, D)
    xs = x_sample.reshape(Bs * L, D)
    mem = mem_prompt.reshape(B * NM, D)

    tq = min(T, 1024)
    tk = min(T, 1024)
    n_fox = (depth + 1) // 2
    pk_all = pv_all = jnp.zeros((n_fox, B, H, T, HEAD_DIM), F32)
    sk_all = sv_all = jnp.zeros((n_fox, Bs, H, L, HEAD_DIM), F32)
    rb_p = min(T, CHUNK)
    head_major = (0, 1, 3, 2, 4)
    ck_heads = jnp.transpose(cache_fox_k, head_major)
    cv_heads = jnp.transpose(cache_fox_v, head_major)

    p_lf, p_S, p_conv, p_mk, p_mv = [], [], [], [], []
    s_lf, s_S, s_conv = [], [], []

    for i in range(depth):
        j = i // 2
        hm = rms_norm_rows(mem, mem_norm_gain[i], BF16)
        wkv = w_mem_kv[i].astype(BF16)
        mk_p = mm(hm, wkv[:, :MEMW], "f32").reshape(B, NM, MEMW)
        mv_p = mm(hm, wkv[:, MEMW:], "f32").reshape(B, NM, MEMW)
        p_mk.append(mk_p)
        p_mv.append(mv_p)

        if i == 0:
            hp = rms_norm_rows(xp, norm_gain[0], BF16)
            hs = rms_norm_rows(xs, norm_gain[0], BF16)
        w_o = w_out[i].astype(BF16)

        if i % 2 == 0:
            wq, wk, wv, wf, wqm, wz = cast_cols(
                w_in_fox, j, [(0, MIX), (MIX, MIX), (2 * MIX, MIX), (QKV, H), (QKV + H, MEMW),
                              (QKV + H + MEMW, MIX + MEMW)])
            bf_pad = _pad_vec(b_forget[j], LANE)

            q = mm(hp, wq, "scale_bf16", scale * LOG2E)
            pk_all, k16 = proj_kv(hp, wk, pk_all, j, n_fox, B, T)
            pv_all, v16 = proj_kv(hp, wv, pv_all, j, n_fox, B, T)
            fg = mm(hp, wf, "f32")
            qm_p = mm(hp, wqm, "scale_bf16", scale)
            g_p = mm(hp, wz, "silu_bf16")
            logf, qa, ka = fox_prep(fg, bf_pad, B, T, min(T, 512))
            mix_p = fox_attn(q, qa, k16, ka, v16, B, T, tq, tk)
            p_lf.append(logf[:, :H].reshape(B, T, H))

            q = mm(hs, wq, "scale_bf16", scale)
            sk_all, k16 = proj_kv(hs, wk, sk_all, j, n_fox, Bs, L)
            sv_all, v16 = proj_kv(hs, wv, sv_all, j, n_fox, Bs, L)
            fg = mm(hs, wf, "f32")
            qm_s = mm(hs, wqm, "scale_bf16", scale)
            g_s = mm(hs, wz, "silu_bf16")
            fgt = fg.reshape(Bs, L, LANE).transpose(0, 2, 1)
            clf_t = cache_fox_logf[j].astype(F32).transpose(0, 2, 1).reshape(Bs * H, P)
            rsum = suffix_sum(clf_t).reshape(Bs, H, P)
            mix_s, logf_s = fox_sample(q, k16, v16, ck_heads, cv_heads, j,
                                       fg, fgt, bf_pad, bf_pad.reshape(LANE, 1), rsum, Bs, L)
            s_lf.append(logf_s[:, :H].reshape(Bs, L, H))
        else:
            wqkv, wa, wb, wqm, wz = cast_cols(
                w_in_gdn, j, [(0, QKV), (QKV, H), (QKV + H, H), (QKV + 2 * H, MEMW),
                              (QKV + 2 * H + MEMW, MIX + MEMW)])
            wab = jnp.concatenate([wa, wb], axis=1)
            alog_pad = _pad_vec(gdn_A_log[j], LANE)
            dtb_pad = _pad_vec(gdn_dt_bias[j], LANE)

            qkv = mm(hp, wqkv, "f32")
            ab = mm(hp, wab, "f32")
            qm_p = mm(hp, wqm, "scale_bf16", scale)
            g_p = mm(hp, wz, "silu_bf16")
            gc, beta, gr = gdn_prep(ab, alog_pad, dtb_pad, B, T, rb_p, min(CHUNK, T))
            mix_p, S, conv = gdn_mixer(qkv, gdn_conv_w[j], gc, beta, gr,
                                       jnp.zeros((1, B, H, HEAD_DIM, HEAD_DIM), F32),
                                       jnp.zeros((1, B, 3, QKV), F32), 0,
                                       gdn_out_norm[j], B, T, rb_p, min(CHUNK, T), H)
            p_S.append(S)
            p_conv.append(conv)

            qkv = mm(hs, wqkv, "f32")
            ab = mm(hs, wab, "f32")
            qm_s = mm(hs, wqm, "scale_bf16", scale)
            g_s = mm(hs, wz, "silu_bf16")
            gc, beta, gr = gdn_prep(ab, alog_pad, dtb_pad, Bs, L, L, L)
            mix_s, S, conv = gdn_mixer(qkv, gdn_conv_w[j], gc, beta, gr, state_gdn_S, state_gdn_conv, j,
                                       gdn_out_norm[j], Bs, L, L, L, H)
            s_S.append(S)
            s_conv.append(conv)

        m_p = mem_attn(qm_p, mk_p, mv_p, T)
        m_s = mem_attn(qm_s, cache_mem_k, cache_mem_v, L, layer=i)
        if i + 1 < depth:
            xp, hp = out_proj(mix_p, m_p, g_p, w_o, xp, norm_gain[i + 1], False)
            xs, hs = out_proj(mix_s, m_s, g_s, w_o, xs, norm_gain[i + 1], False)
        else:
            y_prompt = out_proj(mix_p, m_p, g_p, w_o, xp, final_norm, True).reshape(B, T, D)
            y_sample = out_proj(mix_s, m_s, g_s, w_o, xs, final_norm, True).reshape(Bs, L, D)

    mem_shape = (depth, B, NM, N_MEM_HEADS, HEAD_DIM)
    return (y_prompt, y_sample,
            jnp.transpose(pk_all, head_major), jnp.transpose(pv_all, head_major), jnp.stack(p_lf), jnp.stack(p_S), jnp.stack(p_conv),
            jnp.stack(p_mk).reshape(mem_shape), jnp.stack(p_mv).reshape(mem_shape),
            jnp.transpose(sk_all, head_major), jnp.transpose(sv_all, head_major), jnp.stack(s_lf), jnp.stack(s_S), jnp.stack(s_conv))
```
